```python
import math
import jax, jax.numpy as jnp
from jax import lax
import numpy as np

D_MODEL = 1024
BATCH = 16
SEQ = 2048
DEPTH = 2

CHUNK = 64
Q_BLOCK = 128

MLA_HEADS = 8
MLA_NOPE = 64
MLA_ROPE = 32
MLA_V = 64
MLA_Q_RANK = 256
MLA_KV_RANK = 128
ROPE_THETA = 10000.0

CONV_GROUPS = 4
CONV_GROUP_WIDTH = 64
CONV_CH = CONV_GROUPS * CONV_GROUP_WIDTH
CONV_WIDTH = 3

SB_HEADS = 4
SB_HEAD_DIM = 64
SB_WIDTH = SB_HEADS * SB_HEAD_DIM

D_MIX = MLA_HEADS * MLA_V + CONV_CH + SB_WIDTH
IN_SIZES = (MLA_Q_RANK, MLA_KV_RANK, MLA_ROPE,
            CONV_CH, CONV_CH, CONV_CH,
            SB_WIDTH, SB_WIDTH, SB_WIDTH)
IN_COLS = sum(IN_SIZES)
IN_SPLITS = tuple(int(v) for v in np.cumsum(IN_SIZES)[:-1])

PEER_HEADS = 8
PEER_N_KEYS = 128
PEER_N_EXPERTS = PEER_N_KEYS * PEER_N_KEYS
PEER_D_KEY = 256
PEER_HALF = PEER_D_KEY // 2
PEER_TOPK = 16
PEER_TOKEN_BLOCK = 128

LN_EPS = 1e-5
RMS_EPS = 1e-6
DEEPNORM_ALPHA = (2.0 * DEPTH) ** 0.25
DEEPNORM_BETA = (8.0 * DEPTH) ** -0.25
NEG_INF = -1e30

kernel_name = "hybrid_mla_conv_stickbreak_peer_encoder"


def _layer_norm(x, gain, bias):
    xf = x.astype(jnp.float32)
    mu = jnp.mean(xf, axis=-1, keepdims=True)
    var = jnp.mean(jnp.square(xf - mu), axis=-1, keepdims=True)
    y = (xf - mu) * lax.rsqrt(var + LN_EPS) * gain.astype(jnp.float32) + bias.astype(jnp.float32)
    return y.astype(x.dtype)


def _modulate(x, shift, scale):
    xf = x.astype(jnp.float32)
    mu = jnp.mean(xf, axis=-1, keepdims=True)
    var = jnp.mean(jnp.square(xf - mu), axis=-1, keepdims=True)
    y = ((xf - mu) * lax.rsqrt(var + LN_EPS)).astype(x.dtype)
    return y * (1.0 + scale[:, None, :]) + shift[:, None, :]


def _rms_norm(x, gain):
    xf = x.astype(jnp.float32)
    y = xf * lax.rsqrt(jnp.mean(jnp.square(xf), axis=-1, keepdims=True) + RMS_EPS)
    return (y * gain.astype(jnp.float32)).astype(x.dtype)


def _rope(x, positions):
    half = x.shape[-1] // 2
    inv_freq = ROPE_THETA ** (-jnp.arange(half, dtype=jnp.float32) / half)
    ang = positions.astype(jnp.float32)[..., None] * inv_freq
    cos = jnp.cos(ang)[:, :, None, :]
    sin = jnp.sin(ang)[:, :, None, :]
    xf = x.astype(jnp.float32)
    x1, x2 = xf[..., :half], xf[..., half:]
    out = jnp.concatenate([x1 * cos - x2 * sin, x2 * cos + x1 * sin], axis=-1)
    return out.astype(x.dtype)


def _mla_attention(q_nope, q_rope, k_nope, k_rope, v):
    S = q_nope.shape[1]
    scale = (MLA_NOPE + MLA_ROPE) ** -0.5
    outs = []
    for i in range(S // Q_BLOCK):
        q0, q1 = i * Q_BLOCK, (i + 1) * Q_BLOCK
        kv_end = q1
        s = (jnp.einsum('bqhd,bkhd->bhqk', q_nope[:, q0:q1], k_nope[:, :kv_end])
             + jnp.einsum('bqhr,bkr->bhqk', q_rope[:, q0:q1], k_rope[:, :kv_end]))
        s = s.astype(jnp.float32) * scale
        q_chunk = (q0 + jnp.arange(Q_BLOCK)) // CHUNK
        k_chunk = jnp.arange(kv_end) // CHUNK
        mask = k_chunk[None, :] <= q_chunk[:, None]
        p = jax.nn.softmax(jnp.where(mask, s, NEG_INF), axis=-1)
        outs.append(jnp.einsum('bhqk,bkhd->bqhd', p.astype(v.dtype), v[:, :kv_end]))
    return jnp.concatenate(outs, axis=1)


def _stick_breaking(q, k, v):
    S = q.shape[1]
    scale = SB_HEAD_DIM ** -0.5
    outs = []
    for i in range(S // Q_BLOCK):
        q0, q1 = i * Q_BLOCK, (i + 1) * Q_BLOCK
        kv_end = q1
        z = jnp.einsum('bqhd,bkhd->bhqk', q[:, q0:q1], k[:, :kv_end]).astype(jnp.float32) * scale
        q_pos = q0 + jnp.arange(Q_BLOCK)
        k_pos = jnp.arange(kv_end)
        strict = k_pos[None, :] < q_pos[:, None]
        log_beta = jax.nn.log_sigmoid(z)
        log_one_minus = jnp.where(strict, jax.nn.log_sigmoid(-z), 0.0)
        after = lax.cumsum(log_one_minus, axis=3, reverse=True) - log_one_minus
        w = jnp.where(strict, jnp.exp(log_beta + after), 0.0)
        outs.append(jnp.einsum('bhqk,bkhd->bqhd', w.astype(v.dtype), v[:, :kv_end]))
    return jnp.concatenate(outs, axis=1)


def _short_conv(h, gate_b, gate_c, conv_w):
    u = gate_c * h
    S = u.shape[1]
    u_pad = jnp.pad(u, ((0, 0), (CONV_WIDTH - 1, 0), (0, 0)))
    y = conv_w[0] * u_pad[:, 0:S]
    for tap in range(1, CONV_WIDTH):
        y = y + conv_w[tap] * u_pad[:, tap:tap + S]
    return gate_b * y


def _hybrid_mixer(h, positions, w_in, q_norm_g, kv_norm_g, w_uq, w_ukv, conv_w, w_o):
    Bb, S, _ = h.shape
    proj = h @ w_in
    q_lat, kv_lat, k_rope, c_b, c_c, c_h, sb_q, sb_k, sb_v = jnp.split(proj, IN_SPLITS, axis=-1)
    q = (_rms_norm(q_lat, q_norm_g) @ w_uq).reshape(Bb, S, MLA_HEADS, MLA_NOPE + MLA_ROPE)
    q_nope = q[..., :MLA_NOPE]
    q_rope = _rope(q[..., MLA_NOPE:], positions)
    kv = (_rms_norm(kv_lat, kv_norm_g) @ w_ukv).reshape(Bb, S, MLA_HEADS, MLA_NOPE + MLA_V)
    k_nope, v = kv[..., :MLA_NOPE], kv[..., MLA_NOPE:]
    k_rope = _rope(k_rope[:, :, None, :], positions)[:, :, 0, :]
    mla_o = _mla_attention(q_nope, q_rope, k_nope, k_rope, v).reshape(Bb, S, MLA_HEADS * MLA_V)
    conv_o = _short_conv(c_h, c_b, c_c, conv_w)
    shp = (Bb, S, SB_HEADS, SB_HEAD_DIM)
    sb_o = _stick_breaking(sb_q.reshape(shp), sb_k.reshape(shp), sb_v.reshape(shp)).reshape(Bb, S, SB_WIDTH)
    return jnp.concatenate([mla_o, conv_o, sb_o], axis=-1) @ w_o


def _peer(h, wq, sub_k1, sub_k2, u_tab, v_tab):
    Bb, S, D = h.shape
    q = (h @ wq).reshape(Bb, S, PEER_HEADS, 2, PEER_HALF)
    s1 = jnp.einsum('bshd,hnd->bshn', q[..., 0, :], sub_k1).astype(jnp.float32)
    s2 = jnp.einsum('bshd,hnd->bshn', q[..., 1, :], sub_k2).astype(jnp.float32)
    v1, i1 = lax.top_k(s1, PEER_TOPK)
    v2, i2 = lax.top_k(s2, PEER_TOPK)
    cand_s = (v1[..., :, None] + v2[..., None, :]).reshape(Bb, S, PEER_HEADS, PEER_TOPK * PEER_TOPK)
    cand_e = (i1[..., :, None] * PEER_N_KEYS + i2[..., None, :]).reshape(Bb, S, PEER_HEADS, PEER_TOPK * PEER_TOPK)
    top_s, pos = lax.top_k(cand_s, PEER_TOPK)
    experts = jnp.take_along_axis(cand_e, pos, axis=-1)
    g = jax.nn.softmax(top_s, axis=-1)
    T = Bb * S
    nblk = T // PEER_TOKEN_BLOCK
    hf = h.reshape(nblk, PEER_TOKEN_BLOCK, D)
    ef = experts.reshape(nblk, PEER_TOKEN_BLOCK, PEER_HEADS * PEER_TOPK)
    gf = g.reshape(nblk, PEER_TOKEN_BLOCK, PEER_HEADS * PEER_TOPK).astype(h.dtype)

    def token_block(args):
        hb, eb, gb = args
        u = jnp.take(u_tab, eb, axis=0)
        act = jax.nn.gelu(jnp.einsum('tkd,td->tk', u, hb), approximate=False)
        vv = jnp.take(v_tab, eb, axis=0)
        return jnp.einsum('tk,tkd->td', gb * act, vv)

    out = lax.map(token_block, (hf, ef, gf))
    return out.reshape(Bb, S, D)


def setup_inputs(seed: int = 0) -> dict:
    key = jax.random.key(seed)
    ks = jax.random.split(key, 24)
    D, L = D_MODEL, DEPTH

    def nrm(k, shape, std):
        return jax.random.normal(k, shape, jnp.float32) * std

    x = nrm(ks[0], (BATCH, SEQ, D), 1.0)
    c = nrm(ks[1], (BATCH, D), 1.0)
    offset = jax.random.randint(ks[2], (BATCH, 1), 0, 4096, dtype=jnp.int32)
    positions = offset + jnp.arange(SEQ, dtype=jnp.int32)[None, :]
    ada_w = nrm(ks[3], (L, D, 6 * D), 0.5 * D ** -0.5)
    ada_b = nrm(ks[4], (L, 6 * D), 0.02)
    w_in = nrm(ks[5], (L, D, IN_COLS), D ** -0.5)
    q_norm_g = 1.0 + nrm(ks[6], (L, MLA_Q_RANK), 0.02)
    kv_norm_g = 1.0 + nrm(ks[7], (L, MLA_KV_RANK), 0.02)
    w_uq = nrm(ks[8], (L, MLA_Q_RANK, MLA_HEADS * (MLA_NOPE + MLA_ROPE)), MLA_Q_RANK ** -0.5)
    w_ukv = nrm(ks[9], (L, MLA_KV_RANK, MLA_HEADS * (MLA_NOPE + MLA_V)), MLA_KV_RANK ** -0.5)
    conv_w = nrm(ks[10], (L, CONV_WIDTH, CONV_CH), CONV_WIDTH ** -0.5)
    w_o = nrm(ks[11], (L, D_MIX, D), DEEPNORM_BETA * D_MIX ** -0.5)
    ln1_g = 1.0 + nrm(ks[12], (L, D), 0.02)
    ln1_b = nrm(ks[13], (L, D), 0.02)
    peer_wq = nrm(ks[14], (L, D, PEER_HEADS * PEER_D_KEY), D ** -0.5)
    peer_k1 = nrm(ks[15], (L, PEER_HEADS, PEER_N_KEYS, PEER_HALF), PEER_HALF ** -0.5)
    peer_k2 = nrm(ks[16], (L, PEER_HEADS, PEER_N_KEYS, PEER_HALF), PEER_HALF ** -0.5)
    peer_u = nrm(ks[17], (L, PEER_N_EXPERTS, D), D ** -0.5)
    peer_v = nrm(ks[18], (L, PEER_N_EXPERTS, D), DEEPNORM_BETA * PEER_HEADS ** -0.5)
    ln2_g = 1.0 + nrm(ks[19], (L, D), 0.02)
    ln2_b = nrm(ks[20], (L, D), 0.02)
    return {"x": x, "c": c, "positions": positions, "ada_w": ada_w, "ada_b": ada_b,
            "w_in": w_in, "q_norm_g": q_norm_g, "kv_norm_g": kv_norm_g, "w_uq": w_uq,
            "w_ukv": w_ukv, "conv_w": conv_w, "w_o": w_o, "ln1_g": ln1_g, "ln1_b": ln1_b,
            "peer_wq": peer_wq, "peer_k1": peer_k1, "peer_k2": peer_k2, "peer_u": peer_u,
            "peer_v": peer_v, "ln2_g": ln2_g, "ln2_b": ln2_b}


def reference(x, c, positions, ada_w, ada_b, w_in, q_norm_g, kv_norm_g, w_uq, w_ukv, conv_w,
              w_o, ln1_g, ln1_b, peer_wq, peer_k1, peer_k2, peer_u, peer_v, ln2_g, ln2_b):
    c_act = jax.nn.silu(c)
    for l in range(DEPTH):
        mod = c_act @ ada_w[l] + ada_b[l]
        sh1, sc1, g1, sh2, sc2, g2 = jnp.split(mod, 6, axis=-1)
        h = _modulate(x, sh1, sc1)
        mix = _hybrid_mixer(h, positions, w_in[l], q_norm_g[l], kv_norm_g[l], w_uq[l], w_ukv[l],
                            conv_w[l], w_o[l])
        x = _layer_norm(DEEPNORM_ALPHA * x + (1.0 + g1)[:, None, :] * mix, ln1_g[l], ln1_b[l])
        h = _modulate(x, sh2, sc2)
        ffn = _peer(h, peer_wq[l], peer_k1[l], peer_k2[l], peer_u[l], peer_v[l])
        x = _layer_norm(DEEPNORM_ALPHA * x + (1.0 + g2)[:, None, :] * ffn, ln2_g[l], ln2_b[l])
    return x
```

```python
import functools
import math

import numpy as np
import jax
import jax.numpy as jnp
from jax import lax
from jax.experimental import pallas as pl
from jax.experimental.pallas import tpu as pltpu

F32 = jnp.float32
BF16 = jnp.bfloat16

D_MODEL = 1024
CHUNK = 64

MLA_HEADS = 8
MLA_NOPE = 64
MLA_ROPE = 32
MLA_V = 64
MLA_Q_RANK = 256
MLA_KV_RANK = 128
ROPE_THETA = 10000.0
HEAD_PAD = 128

CONV_CH = 256
CONV_WIDTH = 3
SB_HEADS = 4
SB_HEAD_DIM = 64
SB_WIDTH = 256

PEER_HEADS = 8
PEER_N_KEYS = 128
PEER_HALF = 128
PEER_TOPK = 16

LN_EPS = 1e-5
RMS_EPS = 1e-6
NEG_INF = -1e30
SQRT_HALF = 0.7071067811865476

LANES = 128
SUBLANES = 8
VMEM_LIMIT = 56 * 1024 * 1024

_O_QLAT, _O_KVLAT, _O_KR = 0, 256, 384
_O_CB, _O_CC, _O_CH = 512, 768, 1024
_O_SQ, _O_SK, _O_SV = 1280, 1536, 1792
IN_PAD = 2048

_NT = (((1,), (1,)), ((), ()))


def _ln_noaffine(x):
    mu = jnp.mean(x, axis=-1, keepdims=True)
    xc = x - mu
    var = jnp.mean(xc * xc, axis=-1, keepdims=True)
    return xc * lax.rsqrt(var + LN_EPS)


def _rms(x, gain):
    return x * lax.rsqrt(jnp.mean(x * x, axis=-1, keepdims=True) + RMS_EPS) * gain


def _params(*sem):
    return pltpu.CompilerParams(dimension_semantics=sem, vmem_limit_bytes=VMEM_LIMIT)


def _adaln_body(c_ref, w_ref, b_ref, o_ref):
    c = c_ref[...]
    ca = c * jax.nn.sigmoid(c)
    o_ref[0] = jnp.dot(ca, w_ref[0], preferred_element_type=F32,
                       precision=lax.Precision.HIGHEST) + b_ref[0]


def _adaln(c, ada_w, ada_b):
    L, D, N = ada_w.shape
    B = c.shape[0]
    tn = 1024
    return pl.pallas_call(
        _adaln_body,
        grid=(L, N // tn),
        in_specs=[pl.BlockSpec((B, D), lambda l, n: (0, 0)),
                  pl.BlockSpec((1, D, tn), lambda l, n: (l, 0, n)),
                  pl.BlockSpec((1, 1, tn), lambda l, n: (l, 0, n))],
        out_specs=pl.BlockSpec((1, B, tn), lambda l, n: (l, 0, n)),
        out_shape=jax.ShapeDtypeStruct((L, B, N), F32),
        compiler_params=_params("arbitrary", "arbitrary"),
        name="adaln",
    )(c, ada_w, ada_b.reshape(L, 1, N))


def _mixer_in_body(pos_ref, x_ref, sh_ref, sc_ref, win_ref, qg_ref, kvg_ref, wuq_ref, wk2_ref,
                   wv_ref, cw_ref, invf_ref,
                   q_ref, k_ref, v_ref, conv_ref, sbq_ref, sbk_ref, sbv_ref, ubuf_ref, *, tm):
    j = pl.program_id(1)
    h = _ln_noaffine(x_ref[...]) * (1.0 + sc_ref[0]) + sh_ref[0]
    proj = jnp.dot(h.astype(BF16), win_ref[...], preferred_element_type=F32)

    qn = _rms(proj[:, _O_QLAT:_O_QLAT + MLA_Q_RANK], qg_ref[...])
    kvn = _rms(proj[:, _O_KVLAT:_O_KVLAT + MLA_KV_RANK], kvg_ref[...])
    kr = proj[:, _O_KR:_O_KR + LANES]
    q = jnp.dot(qn.astype(BF16), wuq_ref[...], preferred_element_type=F32)
    kcat = jnp.concatenate([kvn, kr], axis=-1).astype(BF16)
    k = jnp.dot(kcat, wk2_ref[...], preferred_element_type=F32)
    v = jnp.dot(kvn.astype(BF16), wv_ref[...], preferred_element_type=F32)
    v_ref[...] = v.astype(BF16)

    ang = pos_ref[...].astype(F32) * invf_ref[...]
    cos = jnp.cos(ang)
    sin = jnp.sin(ang)
    lane = lax.broadcasted_iota(jnp.int32, (1, LANES), 1)
    half = MLA_ROPE // 2
    first = (lane >= MLA_NOPE) & (lane < MLA_NOPE + half)
    second = (lane >= MLA_NOPE + half) & (lane < MLA_NOPE + MLA_ROPE)
    s_prev = jnp.where(second, sin, 0.0)
    s_next = jnp.where(first, -sin, 0.0)
    qscale = (MLA_NOPE + MLA_ROPE) ** -0.5
    for hb in range(MLA_HEADS):
        sl = slice(hb * HEAD_PAD, (hb + 1) * HEAD_PAD)
        for src, dst, scale in ((q, q_ref, qscale), (k, k_ref, None)):
            xb = src[:, sl]
            y = (xb * cos + pltpu.roll(xb, half, 1) * s_prev
                 + pltpu.roll(xb, LANES - half, 1) * s_next)
            if scale is not None:
                y = y * scale
            dst[:, sl] = y.astype(BF16)

    cb = proj[:, _O_CB:_O_CB + CONV_CH]
    u = proj[:, _O_CC:_O_CC + CONV_CH] * proj[:, _O_CH:_O_CH + CONV_CH]

    @pl.when(j == 0)
    def _():
        ubuf_ref[0:SUBLANES, :] = jnp.zeros((SUBLANES, CONV_CH), F32)

    ubuf_ref[SUBLANES:SUBLANES + tm, :] = u
    y = (cw_ref[0:1, :] * ubuf_ref[SUBLANES - 2:SUBLANES - 2 + tm, :]
         + cw_ref[1:2, :] * ubuf_ref[SUBLANES - 1:SUBLANES - 1 + tm, :]
         + cw_ref[2:3, :] * u)
    conv_ref[...] = (cb * y).astype(BF16)
    ubuf_ref[0:SUBLANES, :] = ubuf_ref[tm:tm + SUBLANES, :]

    sbq_ref[...] = (proj[:, _O_SQ:_O_SQ + SB_WIDTH] * (SB_HEAD_DIM ** -0.5)).astype(BF16)
    sbk_ref[...] = proj[:, _O_SK:_O_SK + SB_WIDTH].astype(BF16)
    sbv_ref[...] = proj[:, _O_SV:_O_SV + SB_WIDTH].astype(BF16)


def _mixer_in(x, pos, sh, sc, lw, B, S, tm):
    T, D = x.shape
    nt = S // tm
    row = lambda b, j: (b * nt + j, 0)
    per_b = lambda b, j: (b, 0, 0)
    const = lambda b, j: (0, 0)
    full = lambda a: pl.BlockSpec(a.shape, const)
    outs = [(MLA_HEADS * HEAD_PAD, BF16), (MLA_HEADS * HEAD_PAD, BF16), (MLA_HEADS * MLA_V, BF16),
            (CONV_CH, BF16), (SB_WIDTH, BF16), (SB_WIDTH, BF16), (SB_WIDTH, BF16)]
    return pl.pallas_call(
        functools.partial(_mixer_in_body, tm=tm),
        grid=(B, nt),
        in_specs=[pl.BlockSpec((tm, 1), row), pl.BlockSpec((tm, D), row),
                  pl.BlockSpec((1, 1, D), per_b), pl.BlockSpec((1, 1, D), per_b),
                  full(lw["w_in"]), full(lw["qg"]), full(lw["kvg"]), full(lw["w_uq"]),
                  full(lw["wk2"]), full(lw["wv"]), full(lw["conv_w"]), full(lw["invf"])],
        out_specs=[pl.BlockSpec((tm, n), row) for n, _ in outs],
        out_shape=[jax.ShapeDtypeStruct((T, n), dt) for n, dt in outs],
        scratch_shapes=[pltpu.VMEM((tm + SUBLANES, CONV_CH), F32)],
        compiler_params=_params("arbitrary", "arbitrary"),
        name="mixer_in",
    )(pos, x, sh, sc, lw["w_in"], lw["qg"], lw["kvg"], lw["w_uq"], lw["wk2"], lw["wv"],
      lw["conv_w"], lw["invf"])


def _mla_body(q_ref, k_ref, v_ref, o_ref, *, S, tq):
    r = lax.broadcasted_iota(jnp.int32, (tq, tq), 0) // CHUNK
    c = lax.broadcasted_iota(jnp.int32, (tq, tq), 1) // CHUNK
    visible = c <= r

    for hh in range(2):
        hs = hh * HEAD_PAD
        vs = hh * MLA_V

        def q_tile(i, _, hs=hs, vs=vs):
            q0 = pl.multiple_of(i * tq, tq)
            q = q_ref[pl.ds(q0, tq), hs:hs + HEAD_PAD]

            def step(k0, carry, masked):
                m, l, acc = carry
                kb = k_ref[pl.ds(k0, tq), hs:hs + HEAD_PAD]
                vb = v_ref[pl.ds(k0, tq), vs:vs + MLA_V]
                s = lax.dot_general(q, kb, _NT, preferred_element_type=F32)
                if masked:
                    s = jnp.where(visible, s, NEG_INF)
                m_new = jnp.maximum(m, jnp.max(s, axis=-1, keepdims=True))
                p = jnp.exp(s - m_new)
                alpha = jnp.exp(m - m_new)
                l = alpha * l + jnp.sum(p, axis=-1, keepdims=True)
                acc = alpha * acc + jnp.dot(p.astype(BF16), vb, preferred_element_type=F32)
                return m_new, l, acc

            init = (jnp.full((tq, 1), NEG_INF, F32), jnp.zeros((tq, 1), F32),
                    jnp.zeros((tq, MLA_V), F32))
            carry = lax.fori_loop(
                0, i, lambda jj, cr: step(pl.multiple_of(jj * tq, tq), cr, False), init)
            _, l, acc = step(q0, carry, True)
            o_ref[pl.ds(q0, tq), vs:vs + MLA_V] = (acc / l).astype(BF16)
            return 0

        lax.fori_loop(0, S // tq, q_tile, 0)


def _mla(q, k, v, B, S, tq):
    T = q.shape[0]
    hp = MLA_HEADS // 2
    return pl.pallas_call(
        functools.partial(_mla_body, S=S, tq=tq),
        grid=(B, hp),
        in_specs=[pl.BlockSpec((S, 2 * HEAD_PAD), lambda b, h: (b, h)),
                  pl.BlockSpec((S, 2 * HEAD_PAD), lambda b, h: (b, h)),
                  pl.BlockSpec((S, 2 * MLA_V), lambda b, h: (b, h))],
        out_specs=pl.BlockSpec((S, 2 * MLA_V), lambda b, h: (b, h)),
        out_shape=jax.ShapeDtypeStruct((T, MLA_HEADS * MLA_V), BF16),
        compiler_params=_params("arbitrary", "arbitrary"),
        name="mla",
    )(q, k, v)


def _softplus(z):
    return jnp.maximum(z, 0.0) + jnp.log1p(jnp.exp(-jnp.abs(z)))


def _stickbreak_body(q_ref, k_ref, v_ref, o_ref, *, S, t):
    ri = lax.broadcasted_iota(jnp.int32, (t, t), 0)
    ci = lax.broadcasted_iota(jnp.int32, (t, t), 1)
    strict = ci < ri
    upper = jnp.where(ri >= ci, 1.0, 0.0).astype(BF16)

    for hh in range(SB_HEADS):
        hs = hh * SB_HEAD_DIM

        def q_tile(i, _, hs=hs):
            q0 = pl.multiple_of(i * t, t)
            q = q_ref[pl.ds(q0, t), hs:hs + SB_HEAD_DIM]

            def step(k0, carry, diag):
                after, acc = carry
                kb = k_ref[pl.ds(k0, t), hs:hs + SB_HEAD_DIM]
                vb = v_ref[pl.ds(k0, t), hs:hs + SB_HEAD_DIM]
                z = lax.dot_general(q, kb, _NT, preferred_element_type=F32)
                sp = _softplus(z)
                if diag:
                    sp = jnp.where(strict, sp, 0.0)
                hi = sp.astype(BF16)
                lo = (sp - hi.astype(F32)).astype(BF16)
                incl = (jnp.dot(hi, upper, preferred_element_type=F32)
                        + jnp.dot(lo, upper, preferred_element_type=F32))
                w = jnp.exp(z - incl - after)
                if diag:
                    w = jnp.where(strict, w, 0.0)
                acc = acc + jnp.dot(w.astype(BF16), vb, preferred_element_type=F32)
                return after + incl[:, 0:1], acc

            carry = step(q0, (jnp.zeros((t, 1), F32), jnp.zeros((t, SB_HEAD_DIM), F32)), True)
            _, acc = lax.fori_loop(
                0, i, lambda jj, cr: step(pl.multiple_of((i - 1 - jj) * t, t), cr, False), carry)
            o_ref[pl.ds(q0, t), hs:hs + SB_HEAD_DIM] = acc.astype(BF16)
            return 0

        lax.fori_loop(0, S // t, q_tile, 0)


def _stickbreak(q, k, v, B, S, t):
    T = q.shape[0]
    spec = pl.BlockSpec((S, SB_WIDTH), lambda b: (b, 0))
    return pl.pallas_call(
        functools.partial(_stickbreak_body, S=S, t=t),
        grid=(B,),
        in_specs=[spec, spec, spec],
        out_specs=spec,
        out_shape=jax.ShapeDtypeStruct((T, SB_WIDTH), BF16),
        compiler_params=_params("arbitrary"),
        name="stickbreak",
    )(q, k, v)


def _oddeven_pairs(n):
    pairs = []
    p = 1
    while p < n:
        k = p
        while k >= 1:
            for j in range(k % p, n - k, 2 * k):
                for i in range(min(k, n - j - k)):
                    if (i + j) // (2 * p) == (i + j + k) // (2 * p):
                        pairs.append((i + j, i + j + k))
            k //= 2
        p *= 2
    return pairs


_SORT16 = _oddeven_pairs(PEER_TOPK)


def _cmpx(xs, i, j):
    a, b = xs[i], xs[j]
    if a is None:
        xs[i], xs[j] = b, None
    elif b is not None:
        xs[i], xs[j] = jnp.maximum(a, b), jnp.minimum(a, b)


def _bitonic_merge(xs):
    n = len(xs)
    d = n // 2
    while d >= 1:
        for i in range(n):
            if (i // d) % 2 == 0:
                _cmpx(xs, i, i + d)
        d //= 2


def _top16_sorted(xs):
    xs = list(xs)
    for i, j in _SORT16:
        _cmpx(xs, i, j)
    for shift in (4, 2, 1):
        ps = [None if x is None else pltpu.roll(x, shift, 0) for x in xs]
        merged = []
        for i in range(PEER_TOPK):
            a, b = xs[i], ps[PEER_TOPK - 1 - i]
            merged.append(b if a is None else a if b is None else jnp.maximum(a, b))
        xs = merged
        _bitonic_merge(xs)
    return xs


def _top16_pair_sums(v1, v2):
    sub = lax.broadcasted_iota(jnp.int32, v1[0].shape, 0)

    def pack(vals):
        out = vals[0]
        for s in range(1, SUBLANES):
            out = jnp.where(sub == s, vals[s], out)
        return out

    b_lo, b_hi = pack(v2[:SUBLANES]), pack(v2[SUBLANES:])
    cands = [v1[0] + b_lo, v1[0] + b_hi]
    for a in range(1, SUBLANES):
        s = v1[a] + b_lo
        lim = PEER_TOPK // (a + 1)
        cands.append(s if lim >= SUBLANES else jnp.where(sub < lim, s, -jnp.inf))
    cands.append(pack(v1[SUBLANES:]) + v2[0])
    cands += [None] * (PEER_TOPK - len(cands))
    return _top16_sorted(cands)


def _mid_body(att_ref, conv_ref, sb_ref, x_ref, g1_ref, sh2_ref, sc2_ref, lng_ref, lnb_ref,
              wo_ref, wq_ref, k1_ref, k2_ref,
              x1_ref, h2_ref, thr_ref, e1_ref, s2_ref, e2_ref, *, alpha):
    cat = jnp.concatenate([att_ref[...], conv_ref[...], sb_ref[...]], axis=-1)
    mix = jnp.dot(cat, wo_ref[...], preferred_element_type=F32)
    x1 = _ln_noaffine(alpha * x_ref[...] + (1.0 + g1_ref[0]) * mix) * lng_ref[...] + lnb_ref[...]
    x1_ref[...] = x1
    h2 = (_ln_noaffine(x1) * (1.0 + sc2_ref[0]) + sh2_ref[0]).astype(BF16)
    h2_ref[...] = h2
    qp = jnp.dot(h2, wq_ref[...], preferred_element_type=F32).astype(BF16)
    _route(qp, k1_ref, k2_ref, thr_ref, e1_ref, s2_ref, e2_ref)


def _route(qp, k1_ref, k2_ref, thr_ref, e1_ref, s2_ref, e2_ref):
    for h in range(PEER_HEADS):
        o = h * 2 * PEER_HALF
        s1 = lax.dot_general(k1_ref[h], qp[:, o:o + PEER_HALF], _NT, preferred_element_type=F32)
        s2 = lax.dot_general(k2_ref[h], qp[:, o + PEER_HALF:o + 2 * PEER_HALF], _NT,
                             preferred_element_type=F32)
        split = lambda s: [s[SUBLANES * i:SUBLANES * (i + 1), :] for i in range(PEER_N_KEYS // SUBLANES)]
        v1 = _top16_sorted(split(s1))
        v2 = _top16_sorted(split(s2))
        top = _top16_pair_sums(v1, v2)
        tau = top[PEER_TOPK - 1][0:1, :]
        z = jnp.ones_like(top[0])
        for t in top[1:]:
            z = z + jnp.exp(t - top[0])
        e1_ref[h] = jnp.exp(s1 - v1[0][0:1, :])
        e2_ref[h] = jnp.exp(s2 - v2[0][0:1, :]) / z[0:1, :]
        s2_ref[h] = s2
        thr = jnp.full(s1.shape, jnp.inf, F32)
        for b in range(PEER_TOPK):
            vb = v2[b][0:1, :]
            thr = jnp.where(s1 + vb >= tau, vb, thr)
        thr_ref[h] = thr


def _mid(att, conv, sb, x, g1, sh2, sc2, lw, B, S, tm, alpha):
    T, D = x.shape
    nt = S // tm
    row = lambda i: (i, 0)
    per_b = lambda i: (i // nt, 0, 0)
    const2 = lambda i: (0, 0)
    const3 = lambda i: (0, 0, 0)
    route = pl.BlockSpec((PEER_HEADS, PEER_N_KEYS, tm), lambda i: (0, 0, i))
    route_shape = jax.ShapeDtypeStruct((PEER_HEADS, PEER_N_KEYS, T), F32)
    return pl.pallas_call(
        functools.partial(_mid_body, alpha=alpha),
        grid=(T // tm,),
        in_specs=[pl.BlockSpec((tm, att.shape[1]), row), pl.BlockSpec((tm, CONV_CH), row),
                  pl.BlockSpec((tm, SB_WIDTH), row), pl.BlockSpec((tm, D), row),
                  pl.BlockSpec((1, 1, D), per_b), pl.BlockSpec((1, 1, D), per_b),
                  pl.BlockSpec((1, 1, D), per_b),
                  pl.BlockSpec((1, D), const2), pl.BlockSpec((1, D), const2),
                  pl.BlockSpec(lw["w_o"].shape, const2), pl.BlockSpec(lw["wq"].shape, const2),
                  pl.BlockSpec(lw["k1"].shape, const3), pl.BlockSpec(lw["k2"].shape, const3)],
        out_specs=[pl.BlockSpec((tm, D), row), pl.BlockSpec((tm, D), row), route, route, route, route],
        out_shape=[jax.ShapeDtypeStruct((T, D), F32), jax.ShapeDtypeStruct((T, D), BF16),
                   route_shape, route_shape, route_shape, route_shape],
        compiler_params=_params("arbitrary"),
        name="mid",
    )(att, conv, sb, x, g1, sh2, sc2, lw["ln1_g"], lw["ln1_b"], lw["w_o"], lw["wq"],
      lw["k1"], lw["k2"])


def _peer_body(h2_ref, u_ref, vt_ref, thr_ref, e1_ref, s2_ref, e2_ref, x1_ref, g2_ref, lng_ref,
               lnb_ref, o_ref, a_ref, w_ref, acc_ref, *, tm, rows, alpha):
    j = pl.program_id(1)

    @pl.when(j == 0)
    def _():
        acc_ref[...] = jnp.zeros_like(acc_ref)

    a_ref[...] = lax.dot_general(u_ref[...], h2_ref[...], _NT, preferred_element_type=F32)

    for rr in range(rows):
        rs = slice(rr * PEER_N_KEYS, (rr + 1) * PEER_N_KEYS)
        for ts in range(tm // LANES):
            ln = slice(ts * LANES, (ts + 1) * LANES)
            g = jnp.zeros((PEER_N_KEYS, LANES), F32)
            for h in range(PEER_HEADS):
                thr = thr_ref[h, rr:rr + 1, ln]
                e1 = e1_ref[h, rr:rr + 1, ln]
                g = g + jnp.where(s2_ref[h, :, ln] >= thr, e2_ref[h, :, ln], 0.0) * e1
            a = a_ref[rs, ln]
            act = 0.5 * a * (1.0 + lax.erf(a * SQRT_HALF))
            w_ref[rs, ln] = (g * act).astype(BF16)
    acc_ref[...] += jnp.dot(vt_ref[...], w_ref[...], preferred_element_type=F32)

    @pl.when(j == pl.num_programs(1) - 1)
    def _():
        ffn = acc_ref[...].T
        y = alpha * x1_ref[...] + (1.0 + g2_ref[0]) * ffn
        o_ref[...] = _ln_noaffine(y) * lng_ref[...] + lnb_ref[...]


def _peer(h2, thr, e1, s2, e2, x1, g2, lw, B, S, tm, rows, alpha):
    T, D = x1.shape
    nt = S // tm
    ne = rows * PEER_N_KEYS
    n_steps = PEER_N_KEYS // rows
    row = lambda i, j: (i, 0)
    const2 = lambda i, j: (0, 0)
    return pl.pallas_call(
        functools.partial(_peer_body, tm=tm, rows=rows, alpha=alpha),
        grid=(T // tm, n_steps),
        in_specs=[pl.BlockSpec((tm, D), row),
                  pl.BlockSpec((ne, D), lambda i, j: (j, 0)),
                  pl.BlockSpec((D, ne), lambda i, j: (0, j)),
                  pl.BlockSpec((PEER_HEADS, rows, tm), lambda i, j: (0, j, i)),
                  pl.BlockSpec((PEER_HEADS, rows, tm), lambda i, j: (0, j, i)),
                  pl.BlockSpec((PEER_HEADS, PEER_N_KEYS, tm), lambda i, j: (0, 0, i)),
                  pl.BlockSpec((PEER_HEADS, PEER_N_KEYS, tm), lambda i, j: (0, 0, i)),
                  pl.BlockSpec((tm, D), row),
                  pl.BlockSpec((1, 1, D), lambda i, j: (i // nt, 0, 0)),
                  pl.BlockSpec((1, D), const2), pl.BlockSpec((1, D), const2)],
        out_specs=pl.BlockSpec((tm, D), row),
        out_shape=jax.ShapeDtypeStruct((T, D), F32),
        scratch_shapes=[pltpu.VMEM((ne, tm), F32), pltpu.VMEM((ne, tm), BF16),
                        pltpu.VMEM((D, tm), F32)],
        compiler_params=_params("arbitrary", "arbitrary"),
        name="peer",
    )(h2, lw["u"], lw["vt"], thr, e1, s2, e2, x1, g2, lw["ln2_g"], lw["ln2_b"])


def _layer_weights(l, w_in, q_norm_g, kv_norm_g, w_uq, w_ukv, conv_w, w_o, ln1_g, ln1_b,
                   peer_wq, peer_k1, peer_k2, peer_u, peer_v, ln2_g, ln2_b):
    D = D_MODEL
    wi = w_in[l]
    pad = jnp.zeros((D, LANES - MLA_ROPE), F32)
    w_in_p = jnp.concatenate([wi[:, :416], pad, wi[:, 416:]], axis=1).astype(BF16)
    qk = MLA_NOPE + MLA_ROPE
    w_uq_p = jnp.pad(w_uq[l].reshape(MLA_Q_RANK, MLA_HEADS, qk),
                     ((0, 0), (0, 0), (0, HEAD_PAD - qk))).reshape(MLA_Q_RANK, -1).astype(BF16)
    ukv = w_ukv[l].reshape(MLA_KV_RANK, MLA_HEADS, MLA_NOPE + MLA_V)
    wk = jnp.pad(ukv[:, :, :MLA_NOPE], ((0, 0), (0, 0), (0, HEAD_PAD - MLA_NOPE)))
    expand = np.zeros((LANES, MLA_HEADS, HEAD_PAD), np.float32)
    for r in range(MLA_ROPE):
        expand[r, :, MLA_NOPE + r] = 1.0
    wk2 = jnp.concatenate([wk, jnp.asarray(expand)], axis=0).reshape(MLA_KV_RANK + LANES, -1).astype(BF16)
    wv = ukv[:, :, MLA_NOPE:].reshape(MLA_KV_RANK, -1).astype(BF16)
    half = MLA_ROPE // 2
    inv_freq = ROPE_THETA ** (-jnp.arange(half, dtype=F32) / half)
    invf = jnp.zeros((LANES,), F32).at[MLA_NOPE:MLA_NOPE + half].set(inv_freq)
    invf = invf.at[MLA_NOPE + half:MLA_NOPE + MLA_ROPE].set(inv_freq).reshape(1, LANES)
    return dict(
        w_in=w_in_p, qg=q_norm_g[l].reshape(1, -1), kvg=kv_norm_g[l].reshape(1, -1),
        w_uq=w_uq_p, wk2=wk2, wv=wv, conv_w=conv_w[l], invf=invf,
        w_o=w_o[l].astype(BF16), ln1_g=ln1_g[l].reshape(1, D), ln1_b=ln1_b[l].reshape(1, D),
        wq=peer_wq[l].astype(BF16), k1=peer_k1[l].astype(BF16), k2=peer_k2[l].astype(BF16),
        u=peer_u[l].astype(BF16), vt=peer_v[l].T.astype(BF16),
        ln2_g=ln2_g[l].reshape(1, D), ln2_b=ln2_b[l].reshape(1, D))


def _forward(x, c, positions, ada_w, ada_b, *weights, tm_in, tq_mla, t_sb, tm_mid, tm_peer, rows):
    B, S, D = x.shape
    T = B * S
    depth = ada_w.shape[0]
    alpha = (2.0 * depth) ** 0.25
    mod = _adaln(c, ada_w, ada_b)
    pos = positions.reshape(T, 1)
    xf = x.reshape(T, D)
    for l in range(depth):
        lw = _layer_weights(l, *weights)
        sh1, sc1, g1, sh2, sc2, g2 = [m.reshape(B, 1, D) for m in jnp.split(mod[l], 6, axis=-1)]
        q, k, v, conv, sbq, sbk, sbv = _mixer_in(xf, pos, sh1, sc1, lw, B, S, tm_in)
        att = _mla(q, k, v, B, S, tq_mla)
        sbo = _stickbreak(sbq, sbk, sbv, B, S, t_sb)
        x1, h2, thr, e1, s2, e2 = _mid(att, conv, sbo, xf, g1, sh2, sc2, lw, B, S, tm_mid, alpha)
        xf = _peer(h2, thr, e1, s2, e2, x1, g2, lw, B, S, tm_peer, rows, alpha)
    return xf.reshape(B, S, D)


def kernel(x, c, positions, ada_w, ada_b, w_in, q_norm_g, kv_norm_g, w_uq, w_ukv, conv_w, w_o,
           ln1_g, ln1_b, peer_wq, peer_k1, peer_k2, peer_u, peer_v, ln2_g, ln2_b):
    return _forward(x, c, positions, ada_w, ada_b, w_in, q_norm_g, kv_norm_g, w_uq, w_ukv, conv_w,
                    w_o, ln1_g, ln1_b, peer_wq, peer_k1, peer_k2, peer_u, peer_v, ln2_g, ln2_b,
                    tm_in=512, tq_mla=256, t_sb=128, tm_mid=256, tm_peer=512, rows=8)
```

```python
import functools
import math

import numpy as np
import jax
import jax.numpy as jnp
from jax import lax
from jax.experimental import pallas as pl
from jax.experimental.pallas import tpu as pltpu

F32 = jnp.float32
BF16 = jnp.bfloat16

D_MODEL = 1024
CHUNK = 64

MLA_HEADS = 8
MLA_NOPE = 64
MLA_ROPE = 32
MLA_V = 64
MLA_Q_RANK = 256
MLA_KV_RANK = 128
ROPE_THETA = 10000.0
HEAD_PAD = 128

CONV_CH = 256
CONV_WIDTH = 3
SB_HEADS = 4
SB_HEAD_DIM = 64
SB_WIDTH = 256

PEER_HEADS = 8
PEER_N_KEYS = 128
PEER_HALF = 128
PEER_TOPK = 16

LN_EPS = 1e-5
RMS_EPS = 1e-6
NEG_INF = -1e30
SQRT_HALF = 0.7071067811865476

LANES = 128
SUBLANES = 8
VMEM_LIMIT = 56 * 1024 * 1024

_O_QLAT, _O_KVLAT, _O_KR = 0, 256, 384
_O_CB, _O_CC, _O_CH = 512, 768, 1024
_O_SQ, _O_SK, _O_SV = 1280, 1536, 1792
IN_PAD = 2048

_NT = (((1,), (1,)), ((), ()))


def _ln_noaffine(x):
    mu = jnp.mean(x, axis=-1, keepdims=True)
    xc = x - mu
    var = jnp.mean(xc * xc, axis=-1, keepdims=True)
    return xc * lax.rsqrt(var + LN_EPS)


def _rms(x, gain):
    return x * lax.rsqrt(jnp.mean(x * x, axis=-1, keepdims=True) + RMS_EPS) * gain


def _params(*sem):
    return pltpu.CompilerParams(dimension_semantics=sem, vmem_limit_bytes=VMEM_LIMIT)


def _adaln_body(c_ref, w_ref, b_ref, o_ref):
    c = c_ref[...]
    ca = c * jax.nn.sigmoid(c)
    o_ref[0] = jnp.dot(ca, w_ref[0], preferred_element_type=F32,
                       precision=lax.Precision.HIGHEST) + b_ref[0]


def _adaln(c, ada_w, ada_b):
    L, D, N = ada_w.shape
    B = c.shape[0]
    tn = 1024
    return pl.pallas_call(
        _adaln_body,
        grid=(L, N // tn),
        in_specs=[pl.BlockSpec((B, D), lambda l, n: (0, 0)),
                  pl.BlockSpec((1, D, tn), lambda l, n: (l, 0, n)),
                  pl.BlockSpec((1, 1, tn), lambda l, n: (l, 0, n))],
        out_specs=pl.BlockSpec((1, B, tn), lambda l, n: (l, 0, n)),
        out_shape=jax.ShapeDtypeStruct((L, B, N), F32),
        compiler_params=_params("arbitrary", "arbitrary"),
        name="adaln",
    )(c, ada_w, ada_b.reshape(L, 1, N))


def _mixer_in_body(pos_ref, x_ref, sh_ref, sc_ref, win_ref, qg_ref, kvg_ref, wuq_ref, wk2_ref,
                   wv_ref, cw_ref, invf_ref,
                   q_ref, k_ref, v_ref, conv_ref, sbq_ref, sbk_ref, sbv_ref, ubuf_ref, *, tm):
    j = pl.program_id(1)
    h = _ln_noaffine(x_ref[...]) * (1.0 + sc_ref[0]) + sh_ref[0]
    proj = jnp.dot(h.astype(BF16), win_ref[...], preferred_element_type=F32)

    qn = _rms(proj[:, _O_QLAT:_O_QLAT + MLA_Q_RANK], qg_ref[...])
    kvn = _rms(proj[:, _O_KVLAT:_O_KVLAT + MLA_KV_RANK], kvg_ref[...])
    kr = proj[:, _O_KR:_O_KR + LANES]
    q = jnp.dot(qn.astype(BF16), wuq_ref[...], preferred_element_type=F32)
    kcat = jnp.concatenate([kvn, kr], axis=-1).astype(BF16)
    k = jnp.dot(kcat, wk2_ref[...], preferred_element_type=F32)
    v = jnp.dot(kvn.astype(BF16), wv_ref[...], preferred_element_type=F32)
    v_ref[...] = v.astype(BF16)

    ang = pos_ref[...].astype(F32) * invf_ref[...]
    cos = jnp.cos(ang)
    sin = jnp.sin(ang)
    lane = lax.broadcasted_iota(jnp.int32, (1, LANES), 1)
    half = MLA_ROPE // 2
    first = (lane >= MLA_NOPE) & (lane < MLA_NOPE + half)
    second = (lane >= MLA_NOPE + half) & (lane < MLA_NOPE + MLA_ROPE)
    s_prev = jnp.where(second, sin, 0.0)
    s_next = jnp.where(first, -sin, 0.0)
    qscale = (MLA_NOPE + MLA_ROPE) ** -0.5
    for hb in range(MLA_HEADS):
        sl = slice(hb * HEAD_PAD, (hb + 1) * HEAD_PAD)
        for src, dst, scale in ((q, q_ref, qscale), (k, k_ref, None)):
            xb = src[:, sl]
            y = (xb * cos + pltpu.roll(xb, half, 1) * s_prev
                 + pltpu.roll(xb, LANES - half, 1) * s_next)
            if scale is not None:
                y = y * scale
            dst[:, sl] = y.astype(BF16)

    cb = proj[:, _O_CB:_O_CB + CONV_CH]
    u = proj[:, _O_CC:_O_CC + CONV_CH] * proj[:, _O_CH:_O_CH + CONV_CH]

    @pl.when(j == 0)
    def _():
        ubuf_ref[0:SUBLANES, :] = jnp.zeros((SUBLANES, CONV_CH), F32)

    ubuf_ref[SUBLANES:SUBLANES + tm, :] = u
    y = (cw_ref[0:1, :] * ubuf_ref[SUBLANES - 2:SUBLANES - 2 + tm, :]
         + cw_ref[1:2, :] * ubuf_ref[SUBLANES - 1:SUBLANES - 1 + tm, :]
         + cw_ref[2:3, :] * u)
    conv_ref[...] = (cb * y).astype(BF16)
    ubuf_ref[0:SUBLANES, :] = ubuf_ref[tm:tm + SUBLANES, :]

    sbq_ref[...] = (proj[:, _O_SQ:_O_SQ + SB_WIDTH] * (SB_HEAD_DIM ** -0.5)).astype(BF16)
    sbk_ref[...] = proj[:, _O_SK:_O_SK + SB_WIDTH].astype(BF16)
    sbv_ref[...] = proj[:, _O_SV:_O_SV + SB_WIDTH].astype(BF16)


def _mixer_in(x, pos, sh, sc, lw, B, S, tm):
    T, D = x.shape
    nt = S // tm
    row = lambda b, j: (b * nt + j, 0)
    per_b = lambda b, j: (b, 0, 0)
    const = lambda b, j: (0, 0)
    full = lambda a: pl.BlockSpec(a.shape, const)
    outs = [(MLA_HEADS * HEAD_PAD, BF16), (MLA_HEADS * HEAD_PAD, BF16), (MLA_HEADS * MLA_V, BF16),
            (CONV_CH, BF16), (SB_WIDTH, BF16), (SB_WIDTH, BF16), (SB_WIDTH, BF16)]
    return pl.pallas_call(
        functools.partial(_mixer_in_body, tm=tm),
        grid=(B, nt),
        in_specs=[pl.BlockSpec((tm, 1), row), pl.BlockSpec((tm, D), row),
                  pl.BlockSpec((1, 1, D), per_b), pl.BlockSpec((1, 1, D), per_b),
                  full(lw["w_in"]), full(lw["qg"]), full(lw["kvg"]), full(lw["w_uq"]),
                  full(lw["wk2"]), full(lw["wv"]), full(lw["conv_w"]), full(lw["invf"])],
        out_specs=[pl.BlockSpec((tm, n), row) for n, _ in outs],
        out_shape=[jax.ShapeDtypeStruct((T, n), dt) for n, dt in outs],
        scratch_shapes=[pltpu.VMEM((tm + SUBLANES, CONV_CH), F32)],
        compiler_params=_params("arbitrary", "arbitrary"),
        name="mixer_in",
    )(pos, x, sh, sc, lw["w_in"], lw["qg"], lw["kvg"], lw["w_uq"], lw["wk2"], lw["wv"],
      lw["conv_w"], lw["invf"])


def _mla_body(q_ref, k_ref, v_ref, o_ref, *, S, tq):
    r = lax.broadcasted_iota(jnp.int32, (tq, tq), 0) // CHUNK
    c = lax.broadcasted_iota(jnp.int32, (tq, tq), 1) // CHUNK
    visible = c <= r

    for hh in range(2):
        hs = slice(hh * HEAD_PAD, (hh + 1) * HEAD_PAD)
        vs = slice(hh * MLA_V, (hh + 1) * MLA_V)
        for i in range(S // tq):
            rows = slice(i * tq, (i + 1) * tq)
            q = q_ref[rows, hs]
            s_d = lax.dot_general(q, k_ref[rows, hs], _NT, preferred_element_type=F32)
            s_d = jnp.where(visible, s_d, NEG_INF)
            m = jnp.max(s_d, axis=-1, keepdims=True)
            if i > 0:
                s_o = lax.dot_general(q, k_ref[0:i * tq, hs], _NT, preferred_element_type=F32)
                m = jnp.maximum(m, jnp.max(s_o, axis=-1, keepdims=True))
            p_d = jnp.exp(s_d - m)
            l = jnp.sum(p_d, axis=-1, keepdims=True)
            acc = jnp.dot(p_d.astype(BF16), v_ref[rows, vs], preferred_element_type=F32)
            if i > 0:
                p_o = jnp.exp(s_o - m)
                l = l + jnp.sum(p_o, axis=-1, keepdims=True)
                acc = acc + jnp.dot(p_o.astype(BF16), v_ref[0:i * tq, vs], preferred_element_type=F32)
            o_ref[rows, vs] = (acc / l).astype(BF16)


def _mla(q, k, v, B, S, tq):
    T = q.shape[0]
    hp = MLA_HEADS // 2
    return pl.pallas_call(
        functools.partial(_mla_body, S=S, tq=tq),
        grid=(B, hp),
        in_specs=[pl.BlockSpec((S, 2 * HEAD_PAD), lambda b, h: (b, h)),
                  pl.BlockSpec((S, 2 * HEAD_PAD), lambda b, h: (b, h)),
                  pl.BlockSpec((S, 2 * MLA_V), lambda b, h: (b, h))],
        out_specs=pl.BlockSpec((S, 2 * MLA_V), lambda b, h: (b, h)),
        out_shape=jax.ShapeDtypeStruct((T, MLA_HEADS * MLA_V), BF16),
        compiler_params=_params("arbitrary", "arbitrary"),
        name="mla",
    )(q, k, v)


def _softplus(z):
    return jnp.maximum(z, 0.0) + jnp.log1p(jnp.exp(-jnp.abs(z)))


def _stickbreak_body(q_ref, k_ref, v_ref, o_ref, *, S, t):
    ri = lax.broadcasted_iota(jnp.int32, (t, t), 0)
    ci = lax.broadcasted_iota(jnp.int32, (t, t), 1)
    strict = ci < ri
    upper = jnp.where(ri >= ci, 1.0, 0.0).astype(BF16)

    heads = [slice(h * SB_HEAD_DIM, (h + 1) * SB_HEAD_DIM) for h in range(SB_HEADS)]

    def block(q0, k0, carry, diag):
        out = []
        for (after, acc), hs in zip(carry, heads):
            q = q_ref[pl.ds(q0, t), hs]
            kb = k_ref[pl.ds(k0, t), hs]
            vb = v_ref[pl.ds(k0, t), hs]
            z = lax.dot_general(q, kb, _NT, preferred_element_type=F32)
            sp = _softplus(z)
            if diag:
                sp = jnp.where(strict, sp, 0.0)
            hi = sp.astype(BF16)
            lo = (sp - hi.astype(F32)).astype(BF16)
            incl = (jnp.dot(hi, upper, preferred_element_type=F32)
                    + jnp.dot(lo, upper, preferred_element_type=F32))
            w = jnp.exp(z - incl - after)
            if diag:
                w = jnp.where(strict, w, 0.0)
            acc = acc + jnp.dot(w.astype(BF16), vb, preferred_element_type=F32)
            out.append((after + incl[:, 0:1], acc))
        return tuple(out)

    def q_tile(i, _):
        q0 = pl.multiple_of(i * t, t)
        init = tuple((jnp.zeros((t, 1), F32), jnp.zeros((t, SB_HEAD_DIM), F32)) for _ in heads)
        carry = block(q0, q0, init, True)
        carry = lax.fori_loop(
            0, i, lambda jj, cr: block(q0, pl.multiple_of((i - 1 - jj) * t, t), cr, False), carry)
        for (_, acc), hs in zip(carry, heads):
            o_ref[pl.ds(q0, t), hs] = acc.astype(BF16)
        return 0

    lax.fori_loop(0, S // t, q_tile, 0)


def _stickbreak(q, k, v, B, S, t):
    T = q.shape[0]
    spec = pl.BlockSpec((S, SB_WIDTH), lambda b: (b, 0))
    return pl.pallas_call(
        functools.partial(_stickbreak_body, S=S, t=t),
        grid=(B,),
        in_specs=[spec, spec, spec],
        out_specs=spec,
        out_shape=jax.ShapeDtypeStruct((T, SB_WIDTH), BF16),
        compiler_params=_params("arbitrary"),
        name="stickbreak",
    )(q, k, v)


def _oddeven_pairs(n):
    pairs = []
    p = 1
    while p < n:
        k = p
        while k >= 1:
            for j in range(k % p, n - k, 2 * k):
                for i in range(min(k, n - j - k)):
                    if (i + j) // (2 * p) == (i + j + k) // (2 * p):
                        pairs.append((i + j, i + j + k))
            k //= 2
        p *= 2
    return pairs


_SORT16 = _oddeven_pairs(PEER_TOPK)


def _cmpx(xs, i, j):
    a, b = xs[i], xs[j]
    if a is None:
        xs[i], xs[j] = b, None
    elif b is not None:
        xs[i], xs[j] = jnp.maximum(a, b), jnp.minimum(a, b)


def _bitonic_merge(xs):
    n = len(xs)
    d = n // 2
    while d >= 1:
        for i in range(n):
            if (i // d) % 2 == 0:
                _cmpx(xs, i, i + d)
        d //= 2


def _top16_sorted(xs):
    xs = list(xs)
    for i, j in _SORT16:
        _cmpx(xs, i, j)
    for shift in (4, 2, 1):
        ps = [None if x is None else pltpu.roll(x, shift, 0) for x in xs]
        merged = []
        for i in range(PEER_TOPK):
            a, b = xs[i], ps[PEER_TOPK - 1 - i]
            merged.append(b if a is None else a if b is None else jnp.maximum(a, b))
        xs = merged
        _bitonic_merge(xs)
    return xs


def _top16_pair_sums(v1, v2):
    sub = lax.broadcasted_iota(jnp.int32, v1[0].shape, 0)

    def pack(vals):
        out = vals[0]
        for s in range(1, SUBLANES):
            out = jnp.where(sub == s, vals[s], out)
        return out

    b_lo, b_hi = pack(v2[:SUBLANES]), pack(v2[SUBLANES:])
    cands = [v1[0] + b_lo, v1[0] + b_hi]
    for a in range(1, SUBLANES):
        s = v1[a] + b_lo
        lim = PEER_TOPK // (a + 1)
        cands.append(s if lim >= SUBLANES else jnp.where(sub < lim, s, -jnp.inf))
    cands.append(pack(v1[SUBLANES:]) + v2[0])
    cands += [None] * (PEER_TOPK - len(cands))
    return _top16_sorted(cands)


def _mid_body(att_ref, conv_ref, sb_ref, x_ref, g1_ref, sh2_ref, sc2_ref, lng_ref, lnb_ref,
              wo_ref, wq_ref, k1_ref, k2_ref,
              x1_ref, h2_ref, cnt_ref, e1_ref, rank_ref, e2_ref, *, alpha):
    cat = jnp.concatenate([att_ref[...], conv_ref[...], sb_ref[...]], axis=-1)
    mix = jnp.dot(cat, wo_ref[...], preferred_element_type=F32)
    x1 = _ln_noaffine(alpha * x_ref[...] + (1.0 + g1_ref[0]) * mix) * lng_ref[...] + lnb_ref[...]
    x1_ref[...] = x1
    h2 = (_ln_noaffine(x1) * (1.0 + sc2_ref[0]) + sh2_ref[0]).astype(BF16)
    h2_ref[...] = h2
    qp = jnp.dot(h2, wq_ref[...], preferred_element_type=F32).astype(BF16)
    _route(qp, k1_ref, k2_ref, cnt_ref, e1_ref, rank_ref, e2_ref)


def _route(qp, k1_ref, k2_ref, cnt_ref, e1_ref, rank_ref, e2_ref):
    for h in range(PEER_HEADS):
        o = h * 2 * PEER_HALF
        s1 = lax.dot_general(k1_ref[h], qp[:, o:o + PEER_HALF], _NT, preferred_element_type=F32)
        s2 = lax.dot_general(k2_ref[h], qp[:, o + PEER_HALF:o + 2 * PEER_HALF], _NT,
                             preferred_element_type=F32)
        split = lambda s: [s[SUBLANES * i:SUBLANES * (i + 1), :] for i in range(PEER_N_KEYS // SUBLANES)]
        v1 = _top16_sorted(split(s1))
        v2 = _top16_sorted(split(s2))
        top = _top16_pair_sums(v1, v2)
        tau = top[PEER_TOPK - 1][0:1, :]
        z = jnp.ones_like(top[0])
        for t in top[1:]:
            z = z + jnp.exp(t - top[0])
        e1_ref[h] = jnp.exp(s1 - v1[0][0:1, :]).astype(BF16)
        e2_ref[h] = (jnp.exp(s2 - v2[0][0:1, :]) * (0.5 / z[0:1, :])).astype(BF16)
        cnt = jnp.zeros(s1.shape, F32)
        rank = jnp.zeros(s2.shape, F32)
        for b in range(PEER_TOPK):
            vb = v2[b][0:1, :]
            cnt = cnt + jnp.where(s1 + vb >= tau, 1.0, 0.0)
            rank = rank + jnp.where(s2 < vb, 1.0, 0.0)
        cnt_ref[h] = cnt.astype(BF16)
        rank_ref[h] = rank.astype(BF16)


def _mid(att, conv, sb, x, g1, sh2, sc2, lw, B, S, tm, alpha):
    T, D = x.shape
    nt = S // tm
    row = lambda i: (i, 0)
    per_b = lambda i: (i // nt, 0, 0)
    const2 = lambda i: (0, 0)
    const3 = lambda i: (0, 0, 0)
    route = pl.BlockSpec((PEER_HEADS, PEER_N_KEYS, tm), lambda i: (0, 0, i))
    route_shape = jax.ShapeDtypeStruct((PEER_HEADS, PEER_N_KEYS, T), BF16)
    return pl.pallas_call(
        functools.partial(_mid_body, alpha=alpha),
        grid=(T // tm,),
        in_specs=[pl.BlockSpec((tm, att.shape[1]), row), pl.BlockSpec((tm, CONV_CH), row),
                  pl.BlockSpec((tm, SB_WIDTH), row), pl.BlockSpec((tm, D), row),
                  pl.BlockSpec((1, 1, D), per_b), pl.BlockSpec((1, 1, D), per_b),
                  pl.BlockSpec((1, 1, D), per_b),
                  pl.BlockSpec((1, D), const2), pl.BlockSpec((1, D), const2),
                  pl.BlockSpec(lw["w_o"].shape, const2), pl.BlockSpec(lw["wq"].shape, const2),
                  pl.BlockSpec(lw["k1"].shape, const3), pl.BlockSpec(lw["k2"].shape, const3)],
        out_specs=[pl.BlockSpec((tm, D), row), pl.BlockSpec((tm, D), row), route, route, route, route],
        out_shape=[jax.ShapeDtypeStruct((T, D), F32), jax.ShapeDtypeStruct((T, D), BF16),
                   route_shape, route_shape, route_shape, route_shape],
        compiler_params=_params("arbitrary"),
        name="mid",
    )(att, conv, sb, x, g1, sh2, sc2, lw["ln1_g"], lw["ln1_b"], lw["w_o"], lw["wq"],
      lw["k1"], lw["k2"])


def _peer_body(h2_ref, u_ref, vt_ref, cnt_ref, e1_ref, rank_ref, e2_ref, x1_ref, g2_ref, lng_ref,
               lnb_ref, o_ref, a_ref, w_ref, acc_ref, *, tm, rows, alpha):
    j = pl.program_id(1)

    @pl.when(j == 0)
    def _():
        acc_ref[...] = jnp.zeros_like(acc_ref)

    a_ref[...] = lax.dot_general(u_ref[...], h2_ref[...], _NT, preferred_element_type=F32)

    for rr in range(rows):
        rs = slice(rr * PEER_N_KEYS, (rr + 1) * PEER_N_KEYS)
        tw = 2 * LANES
        for ts in range(tm // tw):
            ln = slice(ts * tw, (ts + 1) * tw)
            g = jnp.zeros((PEER_N_KEYS, tw), BF16)
            for h in range(PEER_HEADS):
                cnt = cnt_ref[h, rr:rr + 1, ln]
                e1 = e1_ref[h, rr:rr + 1, ln]
                e2 = e2_ref[h, :, ln]
                g = g + jnp.where(rank_ref[h, :, ln] < cnt, e2, jnp.zeros_like(e2)) * e1
            a = a_ref[rs, ln]
            act = a * (1.0 + lax.erf(a * SQRT_HALF))
            w_ref[rs, ln] = act.astype(BF16) * g
    acc_ref[...] += jnp.dot(vt_ref[...], w_ref[...], preferred_element_type=F32)

    @pl.when(j == pl.num_programs(1) - 1)
    def _():
        ffn = acc_ref[...].T
        y = alpha * x1_ref[...] + (1.0 + g2_ref[0]) * ffn
        o_ref[...] = _ln_noaffine(y) * lng_ref[...] + lnb_ref[...]


def _peer(h2, cnt, e1, rank, e2, x1, g2, lw, B, S, tm, rows, alpha):
    T, D = x1.shape
    nt = S // tm
    ne = rows * PEER_N_KEYS
    n_steps = PEER_N_KEYS // rows
    row = lambda i, j: (i, 0)
    const2 = lambda i, j: (0, 0)
    return pl.pallas_call(
        functools.partial(_peer_body, tm=tm, rows=rows, alpha=alpha),
        grid=(T // tm, n_steps),
        in_specs=[pl.BlockSpec((tm, D), row),
                  pl.BlockSpec((ne, D), lambda i, j: (j, 0)),
                  pl.BlockSpec((D, ne), lambda i, j: (0, j)),
                  pl.BlockSpec((PEER_HEADS, rows, tm), lambda i, j: (0, j, i)),
                  pl.BlockSpec((PEER_HEADS, rows, tm), lambda i, j: (0, j, i)),
                  pl.BlockSpec((PEER_HEADS, PEER_N_KEYS, tm), lambda i, j: (0, 0, i)),
                  pl.BlockSpec((PEER_HEADS, PEER_N_KEYS, tm), lambda i, j: (0, 0, i)),
                  pl.BlockSpec((tm, D), row),
                  pl.BlockSpec((1, 1, D), lambda i, j: (i // nt, 0, 0)),
                  pl.BlockSpec((1, D), const2), pl.BlockSpec((1, D), const2)],
        out_specs=pl.BlockSpec((tm, D), row),
        out_shape=jax.ShapeDtypeStruct((T, D), F32),
        scratch_shapes=[pltpu.VMEM((ne, tm), F32), pltpu.VMEM((ne, tm), BF16),
                        pltpu.VMEM((D, tm), F32)],
        compiler_params=_params("arbitrary", "arbitrary"),
        name="peer",
    )(h2, lw["u"], lw["vt"], cnt, e1, rank, e2, x1, g2, lw["ln2_g"], lw["ln2_b"])


def _layer_weights(l, w_in, q_norm_g, kv_norm_g, w_uq, w_ukv, conv_w, w_o, ln1_g, ln1_b,
                   peer_wq, peer_k1, peer_k2, peer_u, peer_v, ln2_g, ln2_b):
    D = D_MODEL
    wi = w_in[l]
    pad = jnp.zeros((D, LANES - MLA_ROPE), F32)
    w_in_p = jnp.concatenate([wi[:, :416], pad, wi[:, 416:]], axis=1).astype(BF16)
    qk = MLA_NOPE + MLA_ROPE
    w_uq_p = jnp.pad(w_uq[l].reshape(MLA_Q_RANK, MLA_HEADS, qk),
                     ((0, 0), (0, 0), (0, HEAD_PAD - qk))).reshape(MLA_Q_RANK, -1).astype(BF16)
    ukv = w_ukv[l].reshape(MLA_KV_RANK, MLA_HEADS, MLA_NOPE + MLA_V)
    wk = jnp.pad(ukv[:, :, :MLA_NOPE], ((0, 0), (0, 0), (0, HEAD_PAD - MLA_NOPE)))
    expand = np.zeros((LANES, MLA_HEADS, HEAD_PAD), np.float32)
    for r in range(MLA_ROPE):
        expand[r, :, MLA_NOPE + r] = 1.0
    wk2 = jnp.concatenate([wk, jnp.asarray(expand)], axis=0).reshape(MLA_KV_RANK + LANES, -1).astype(BF16)
    wv = ukv[:, :, MLA_NOPE:].reshape(MLA_KV_RANK, -1).astype(BF16)
    half = MLA_ROPE // 2
    inv_freq = ROPE_THETA ** (-jnp.arange(half, dtype=F32) / half)
    invf = jnp.zeros((LANES,), F32).at[MLA_NOPE:MLA_NOPE + half].set(inv_freq)
    invf = invf.at[MLA_NOPE + half:MLA_NOPE + MLA_ROPE].set(inv_freq).reshape(1, LANES)
    return dict(
        w_in=w_in_p, qg=q_norm_g[l].reshape(1, -1), kvg=kv_norm_g[l].reshape(1, -1),
        w_uq=w_uq_p, wk2=wk2, wv=wv, conv_w=conv_w[l], invf=invf,
        w_o=w_o[l].astype(BF16), ln1_g=ln1_g[l].reshape(1, D), ln1_b=ln1_b[l].reshape(1, D),
        wq=peer_wq[l].astype(BF16), k1=peer_k1[l].astype(BF16), k2=peer_k2[l].astype(BF16),
        u=peer_u[l].astype(BF16), vt=peer_v[l].T.astype(BF16),
        ln2_g=ln2_g[l].reshape(1, D), ln2_b=ln2_b[l].reshape(1, D))


def _forward(x, c, positions, ada_w, ada_b, *weights, tm_in, tq_mla, t_sb, tm_mid, tm_peer, rows):
    B, S, D = x.shape
    T = B * S
    depth = ada_w.shape[0]
    alpha = (2.0 * depth) ** 0.25
    mod = _adaln(c, ada_w, ada_b)
    pos = positions.reshape(T, 1)
    xf = x.reshape(T, D)
    for l in range(depth):
        lw = _layer_weights(l, *weights)
        sh1, sc1, g1, sh2, sc2, g2 = [m.reshape(B, 1, D) for m in jnp.split(mod[l], 6, axis=-1)]
        q, k, v, conv, sbq, sbk, sbv = _mixer_in(xf, pos, sh1, sc1, lw, B, S, tm_in)
        att = _mla(q, k, v, B, S, tq_mla)
        sbo = _stickbreak(sbq, sbk, sbv, B, S, t_sb)
        x1, h2, cnt, e1, rank, e2 = _mid(att, conv, sbo, xf, g1, sh2, sc2, lw, B, S, tm_mid, alpha)
        xf = _peer(h2, cnt, e1, rank, e2, x1, g2, lw, B, S, tm_peer, rows, alpha)
    return xf.reshape(B, S, D)


def kernel(x, c, positions, ada_w, ada_b, w_in, q_norm_g, kv_norm_g, w_uq, w_ukv, conv_w, w_o,
           ln1_g, ln1_b, peer_wq, peer_k1, peer_k2, peer_u, peer_v, ln2_g, ln2_b):
    return _forward(x, c, positions, ada_w, ada_b, w_in, q_norm_g, kv_norm_g, w_uq, w_ukv, conv_w,
                    w_o, ln1_g, ln1_b, peer_wq, peer_k1, peer_k2, peer_u, peer_v, ln2_g, ln2_b,
                    tm_in=512, tq_mla=256, t_sb=256, tm_mid=256, tm_peer=512, rows=16)
```

```python
import functools
import math

import numpy as np
import jax
import jax.numpy as jnp
from jax import lax
from jax.experimental import pallas as pl
from jax.experimental.pallas import tpu as pltpu

F32 = jnp.float32
BF16 = jnp.bfloat16

D_MODEL = 1024
CHUNK = 64

MLA_HEADS = 8
MLA_NOPE = 64
MLA_ROPE = 32
MLA_V = 64
MLA_Q_RANK = 256
MLA_KV_RANK = 128
ROPE_THETA = 10000.0
HEAD_PAD = 128

CONV_CH = 256
CONV_WIDTH = 3
SB_HEADS = 4
SB_HEAD_DIM = 64
SB_WIDTH = 256

PEER_HEADS = 8
PEER_N_KEYS = 128
PEER_HALF = 128
PEER_TOPK = 16

LN_EPS = 1e-5
RMS_EPS = 1e-6
NEG_INF = -1e30
SQRT_HALF = 0.7071067811865476

LANES = 128
SUBLANES = 8
VMEM_LIMIT = 56 * 1024 * 1024

_O_QLAT, _O_KVLAT, _O_KR = 0, 256, 384
_O_CB, _O_CC, _O_CH = 512, 768, 1024
_O_SQ, _O_SK, _O_SV = 1280, 1536, 1792
IN_PAD = 2048

_NT = (((1,), (1,)), ((), ()))


def _ln_noaffine(x):
    mu = jnp.mean(x, axis=-1, keepdims=True)
    xc = x - mu
    var = jnp.mean(xc * xc, axis=-1, keepdims=True)
    return xc * lax.rsqrt(var + LN_EPS)


def _rms(x, gain):
    return x * lax.rsqrt(jnp.mean(x * x, axis=-1, keepdims=True) + RMS_EPS) * gain


def _params(*sem):
    return pltpu.CompilerParams(dimension_semantics=sem, vmem_limit_bytes=VMEM_LIMIT)


def _adaln_body(c_ref, w_ref, b_ref, o_ref):
    c = c_ref[...]
    ca = c * jax.nn.sigmoid(c)
    o_ref[0] = jnp.dot(ca, w_ref[0], preferred_element_type=F32,
                       precision=lax.Precision.HIGHEST) + b_ref[0]


def _adaln(c, ada_w, ada_b):
    L, D, N = ada_w.shape
    B = c.shape[0]
    tn = 1024
    return pl.pallas_call(
        _adaln_body,
        grid=(L, N // tn),
        in_specs=[pl.BlockSpec((B, D), lambda l, n: (0, 0)),
                  pl.BlockSpec((1, D, tn), lambda l, n: (l, 0, n)),
                  pl.BlockSpec((1, 1, tn), lambda l, n: (l, 0, n))],
        out_specs=pl.BlockSpec((1, B, tn), lambda l, n: (l, 0, n)),
        out_shape=jax.ShapeDtypeStruct((L, B, N), F32),
        compiler_params=_params("arbitrary", "arbitrary"),
        name="adaln",
    )(c, ada_w, ada_b.reshape(L, 1, N))


def _mixer_in_body(pos_ref, x_ref, sh_ref, sc_ref, win_ref, qg_ref, kvg_ref, wuq_ref, wk2_ref,
                   wv_ref, cw_ref, invf_ref,
                   q_ref, k_ref, v_ref, conv_ref, sbq_ref, sbk_ref, sbv_ref, ubuf_ref, *, tm):
    j = pl.program_id(1)
    h = _ln_noaffine(x_ref[...]) * (1.0 + sc_ref[0]) + sh_ref[0]
    proj = jnp.dot(h.astype(BF16), win_ref[...], preferred_element_type=F32)

    qn = _rms(proj[:, _O_QLAT:_O_QLAT + MLA_Q_RANK], qg_ref[...])
    kvn = _rms(proj[:, _O_KVLAT:_O_KVLAT + MLA_KV_RANK], kvg_ref[...])
    kr = proj[:, _O_KR:_O_KR + LANES]
    q = jnp.dot(qn.astype(BF16), wuq_ref[...], preferred_element_type=F32)
    kcat = jnp.concatenate([kvn, kr], axis=-1).astype(BF16)
    k = jnp.dot(kcat, wk2_ref[...], preferred_element_type=F32)
    v = jnp.dot(kvn.astype(BF16), wv_ref[...], preferred_element_type=F32)
    v_ref[...] = v.astype(BF16)

    ang = pos_ref[...].astype(F32) * invf_ref[...]
    cos = jnp.cos(ang)
    sin = jnp.sin(ang)
    lane = lax.broadcasted_iota(jnp.int32, (1, LANES), 1)
    half = MLA_ROPE // 2
    first = (lane >= MLA_NOPE) & (lane < MLA_NOPE + half)
    second = (lane >= MLA_NOPE + half) & (lane < MLA_NOPE + MLA_ROPE)
    s_prev = jnp.where(second, sin, 0.0)
    s_next = jnp.where(first, -sin, 0.0)
    qscale = (MLA_NOPE + MLA_ROPE) ** -0.5
    for hb in range(MLA_HEADS):
        sl = slice(hb * HEAD_PAD, (hb + 1) * HEAD_PAD)
        for src, dst, scale in ((q, q_ref, qscale), (k, k_ref, None)):
            xb = src[:, sl]
            y = (xb * cos + pltpu.roll(xb, half, 1) * s_prev
                 + pltpu.roll(xb, LANES - half, 1) * s_next)
            if scale is not None:
                y = y * scale
            dst[:, sl] = y.astype(BF16)

    cb = proj[:, _O_CB:_O_CB + CONV_CH]
    u = proj[:, _O_CC:_O_CC + CONV_CH] * proj[:, _O_CH:_O_CH + CONV_CH]

    @pl.when(j == 0)
    def _():
        ubuf_ref[0:SUBLANES, :] = jnp.zeros((SUBLANES, CONV_CH), F32)

    ubuf_ref[SUBLANES:SUBLANES + tm, :] = u
    y = (cw_ref[0:1, :] * ubuf_ref[SUBLANES - 2:SUBLANES - 2 + tm, :]
         + cw_ref[1:2, :] * ubuf_ref[SUBLANES - 1:SUBLANES - 1 + tm, :]
         + cw_ref[2:3, :] * u)
    conv_ref[...] = (cb * y).astype(BF16)
    ubuf_ref[0:SUBLANES, :] = ubuf_ref[tm:tm + SUBLANES, :]

    for hh in range(SB_HEADS):
        o = hh * SB_HEAD_DIM
        sbq_ref[hh] = (proj[:, _O_SQ + o:_O_SQ + o + SB_HEAD_DIM] * (SB_HEAD_DIM ** -0.5)).astype(BF16)
        sbk_ref[hh] = proj[:, _O_SK + o:_O_SK + o + SB_HEAD_DIM].astype(BF16)
        sbv_ref[hh] = proj[:, _O_SV + o:_O_SV + o + SB_HEAD_DIM].astype(BF16)


def _mixer_in(x, pos, sh, sc, lw, B, S, tm):
    T, D = x.shape
    nt = S // tm
    row = lambda b, j: (b * nt + j, 0)
    per_b = lambda b, j: (b, 0, 0)
    const = lambda b, j: (0, 0)
    full = lambda a: pl.BlockSpec(a.shape, const)
    outs = [(MLA_HEADS * HEAD_PAD, BF16), (MLA_HEADS * HEAD_PAD, BF16), (MLA_HEADS * MLA_V, BF16),
            (CONV_CH, BF16)]
    sb_spec = pl.BlockSpec((SB_HEADS, tm, SB_HEAD_DIM), lambda b, j: (0, b * nt + j, 0))
    sb_shape = jax.ShapeDtypeStruct((SB_HEADS, T, SB_HEAD_DIM), BF16)
    return pl.pallas_call(
        functools.partial(_mixer_in_body, tm=tm),
        grid=(B, nt),
        in_specs=[pl.BlockSpec((tm, 1), row), pl.BlockSpec((tm, D), row),
                  pl.BlockSpec((1, 1, D), per_b), pl.BlockSpec((1, 1, D), per_b),
                  full(lw["w_in"]), full(lw["qg"]), full(lw["kvg"]), full(lw["w_uq"]),
                  full(lw["wk2"]), full(lw["wv"]), full(lw["conv_w"]), full(lw["invf"])],
        out_specs=[pl.BlockSpec((tm, n), row) for n, _ in outs] + [sb_spec] * 3,
        out_shape=[jax.ShapeDtypeStruct((T, n), dt) for n, dt in outs] + [sb_shape] * 3,
        scratch_shapes=[pltpu.VMEM((tm + SUBLANES, CONV_CH), F32)],
        compiler_params=_params("arbitrary", "arbitrary"),
        name="mixer_in",
    )(pos, x, sh, sc, lw["w_in"], lw["qg"], lw["kvg"], lw["w_uq"], lw["wk2"], lw["wv"],
      lw["conv_w"], lw["invf"])


def _mla_body(q_ref, k_ref, v_ref, o_ref, *, S, tq):
    r = lax.broadcasted_iota(jnp.int32, (tq, tq), 0) // CHUNK
    c = lax.broadcasted_iota(jnp.int32, (tq, tq), 1) // CHUNK
    visible = c <= r

    for hh in range(2):
        hs = slice(hh * HEAD_PAD, (hh + 1) * HEAD_PAD)
        vs = slice(hh * MLA_V, (hh + 1) * MLA_V)
        for i in range(S // tq):
            rows = slice(i * tq, (i + 1) * tq)
            q = q_ref[rows, hs]
            s_d = lax.dot_general(q, k_ref[rows, hs], _NT, preferred_element_type=F32)
            s_d = jnp.where(visible, s_d, NEG_INF)
            m = jnp.max(s_d, axis=-1, keepdims=True)
            if i > 0:
                s_o = lax.dot_general(q, k_ref[0:i * tq, hs], _NT, preferred_element_type=F32)
                m = jnp.maximum(m, jnp.max(s_o, axis=-1, keepdims=True))
            p_d = jnp.exp(s_d - m)
            l = jnp.sum(p_d, axis=-1, keepdims=True)
            acc = jnp.dot(p_d.astype(BF16), v_ref[rows, vs], preferred_element_type=F32)
            if i > 0:
                p_o = jnp.exp(s_o - m)
                l = l + jnp.sum(p_o, axis=-1, keepdims=True)
                acc = acc + jnp.dot(p_o.astype(BF16), v_ref[0:i * tq, vs], preferred_element_type=F32)
            o_ref[rows, vs] = (acc / l).astype(BF16)


def _mla(q, k, v, B, S, tq):
    T = q.shape[0]
    hp = MLA_HEADS // 2
    return pl.pallas_call(
        functools.partial(_mla_body, S=S, tq=tq),
        grid=(B, hp),
        in_specs=[pl.BlockSpec((S, 2 * HEAD_PAD), lambda b, h: (b, h)),
                  pl.BlockSpec((S, 2 * HEAD_PAD), lambda b, h: (b, h)),
                  pl.BlockSpec((S, 2 * MLA_V), lambda b, h: (b, h))],
        out_specs=pl.BlockSpec((S, 2 * MLA_V), lambda b, h: (b, h)),
        out_shape=jax.ShapeDtypeStruct((T, MLA_HEADS * MLA_V), BF16),
        compiler_params=_params("arbitrary", "arbitrary"),
        name="mla",
    )(q, k, v)


def _softplus(z):
    return jnp.maximum(z, 0.0) + jnp.log1p(jnp.exp(-jnp.abs(z)))


def _stickbreak_body(q_ref, k_ref, v_ref, o_ref, *, S, t):
    ri = lax.broadcasted_iota(jnp.int32, (t, t), 0)
    ci = lax.broadcasted_iota(jnp.int32, (t, t), 1)
    strict = ci < ri
    upper = jnp.where(ri >= ci, 1.0, 0.0).astype(BF16)

    for i in range(S // t):
        rows = slice(i * t, (i + 1) * t)
        z = lax.dot_general(q_ref[0, rows, :], k_ref[0, 0:(i + 1) * t, :], _NT,
                            preferred_element_type=F32)
        sp = _softplus(z)
        after = jnp.zeros((t, 1), F32)
        ws = [None] * (i + 1)
        for j in range(i, -1, -1):
            cols = slice(j * t, (j + 1) * t)
            sp_j = sp[:, cols]
            if j == i:
                sp_j = jnp.where(strict, sp_j, 0.0)
            hi = sp_j.astype(BF16)
            lo = (sp_j - hi.astype(F32)).astype(BF16)
            incl = (jnp.dot(hi, upper, preferred_element_type=F32)
                    + jnp.dot(lo, upper, preferred_element_type=F32))
            w = jnp.exp(z[:, cols] - incl - after)
            if j == i:
                w = jnp.where(strict, w, 0.0)
            ws[j] = w.astype(BF16)
            after = after + incl[:, 0:1]
        w_all = ws[0] if i == 0 else jnp.concatenate(ws, axis=1)
        o = jnp.dot(w_all, v_ref[0, 0:(i + 1) * t, :], preferred_element_type=F32)
        o_ref[0, rows, :] = o.astype(BF16)


def _stickbreak(q, k, v, B, S, t):
    T = q.shape[1]
    spec = pl.BlockSpec((1, S, SB_HEAD_DIM), lambda b, h: (h, b, 0))
    return pl.pallas_call(
        functools.partial(_stickbreak_body, S=S, t=t),
        grid=(B, SB_HEADS),
        in_specs=[spec, spec, spec],
        out_specs=spec,
        out_shape=jax.ShapeDtypeStruct((SB_HEADS, T, SB_HEAD_DIM), BF16),
        compiler_params=_params("arbitrary", "arbitrary"),
        name="stickbreak",
    )(q, k, v)


def _oddeven_pairs(n):
    pairs = []
    p = 1
    while p < n:
        k = p
        while k >= 1:
            for j in range(k % p, n - k, 2 * k):
                for i in range(min(k, n - j - k)):
                    if (i + j) // (2 * p) == (i + j + k) // (2 * p):
                        pairs.append((i + j, i + j + k))
            k //= 2
        p *= 2
    return pairs


_SORT16 = _oddeven_pairs(PEER_TOPK)


def _cmpx(xs, i, j):
    a, b = xs[i], xs[j]
    if a is None:
        xs[i], xs[j] = b, None
    elif b is not None:
        xs[i], xs[j] = jnp.maximum(a, b), jnp.minimum(a, b)


def _bitonic_merge(xs):
    n = len(xs)
    d = n // 2
    while d >= 1:
        for i in range(n):
            if (i // d) % 2 == 0:
                _cmpx(xs, i, i + d)
        d //= 2


def _top16_sorted(xs):
    xs = list(xs)
    for i, j in _SORT16:
        _cmpx(xs, i, j)
    for shift in (4, 2, 1):
        ps = [None if x is None else pltpu.roll(x, shift, 0) for x in xs]
        merged = []
        for i in range(PEER_TOPK):
            a, b = xs[i], ps[PEER_TOPK - 1 - i]
            merged.append(b if a is None else a if b is None else jnp.maximum(a, b))
        xs = merged
        _bitonic_merge(xs)
    return xs


def _top16_pair_sums(v1, v2):
    sub = lax.broadcasted_iota(jnp.int32, v1[0].shape, 0)

    def pack(vals):
        out = vals[0]
        for s in range(1, SUBLANES):
            out = jnp.where(sub == s, vals[s], out)
        return out

    b_lo, b_hi = pack(v2[:SUBLANES]), pack(v2[SUBLANES:])
    cands = [v1[0] + b_lo, v1[0] + b_hi]
    for a in range(1, SUBLANES):
        s = v1[a] + b_lo
        lim = PEER_TOPK // (a + 1)
        cands.append(s if lim >= SUBLANES else jnp.where(sub < lim, s, -jnp.inf))
    cands.append(pack(v1[SUBLANES:]) + v2[0])
    cands += [None] * (PEER_TOPK - len(cands))
    return _top16_sorted(cands)


def _count_prefix(pred, vals):
    n = len(vals)
    bits = []

    def pivot(level_vals, k):
        if k == len(bits):
            return level_vals[0]
        half = len(level_vals) // 2
        return jnp.where(bits[k], pivot(level_vals[half:], k + 1), pivot(level_vals[:half], k + 1))

    step = n // 2
    while step >= 1:
        cands = [vals[base + step - 1] for base in range(0, n, 2 * step)]
        bits.append(pred(pivot(cands, 0)))
        step //= 2
    count = jnp.where(bits[0], float(n // 2), 0.0)
    for k, b in enumerate(bits[1:], start=1):
        count = count + jnp.where(b, float(n >> (k + 1)), 0.0)
    return jnp.where(pred(vals[n - 1]), float(n), count)


def _mid_body(att_ref, conv_ref, sb_ref, x_ref, g1_ref, sh2_ref, sc2_ref, lng_ref, lnb_ref,
              wo_ref, wq_ref, k1_ref, k2_ref,
              x1_ref, h2_ref, cnt_ref, e1_ref, rank_ref, e2_ref, *, alpha):
    cat = jnp.concatenate([att_ref[...], conv_ref[...]] + [sb_ref[h] for h in range(SB_HEADS)], axis=-1)
    mix = jnp.dot(cat, wo_ref[...], preferred_element_type=F32)
    x1 = _ln_noaffine(alpha * x_ref[...] + (1.0 + g1_ref[0]) * mix) * lng_ref[...] + lnb_ref[...]
    x1_ref[...] = x1
    h2 = (_ln_noaffine(x1) * (1.0 + sc2_ref[0]) + sh2_ref[0]).astype(BF16)
    h2_ref[...] = h2
    qp = jnp.dot(h2, wq_ref[...], preferred_element_type=F32).astype(BF16)
    _route(qp, k1_ref, k2_ref, cnt_ref, e1_ref, rank_ref, e2_ref)


def _route(qp, k1_ref, k2_ref, cnt_ref, e1_ref, rank_ref, e2_ref):
    for h in range(PEER_HEADS):
        o = h * 2 * PEER_HALF
        s1 = lax.dot_general(k1_ref[h], qp[:, o:o + PEER_HALF], _NT, preferred_element_type=F32)
        s2 = lax.dot_general(k2_ref[h], qp[:, o + PEER_HALF:o + 2 * PEER_HALF], _NT,
                             preferred_element_type=F32)
        split = lambda s: [s[SUBLANES * i:SUBLANES * (i + 1), :] for i in range(PEER_N_KEYS // SUBLANES)]
        v1 = _top16_sorted(split(s1))
        v2 = _top16_sorted(split(s2))
        top = _top16_pair_sums(v1, v2)
        tau = top[PEER_TOPK - 1][0:1, :]
        z = jnp.ones_like(top[0])
        for t in top[1:]:
            z = z + jnp.exp(t - top[0])
        e1_ref[h] = jnp.exp(s1 - v1[0][0:1, :]).astype(BF16)
        e2_ref[h] = (jnp.exp(s2 - v2[0][0:1, :]) * (0.5 / z[0:1, :])).astype(BF16)
        col = [v[0:1, :] for v in v2]
        cnt_ref[h] = _count_prefix(lambda vb: s1 + vb >= tau, col).astype(BF16)
        rank_ref[h] = _count_prefix(lambda vb: s2 < vb, col).astype(BF16)


def _mid(att, conv, sb, x, g1, sh2, sc2, lw, B, S, tm, alpha):
    T, D = x.shape
    nt = S // tm
    row = lambda i: (i, 0)
    per_b = lambda i: (i // nt, 0, 0)
    const2 = lambda i: (0, 0)
    const3 = lambda i: (0, 0, 0)
    route = pl.BlockSpec((PEER_HEADS, PEER_N_KEYS, tm), lambda i: (0, 0, i))
    route_shape = jax.ShapeDtypeStruct((PEER_HEADS, PEER_N_KEYS, T), BF16)
    return pl.pallas_call(
        functools.partial(_mid_body, alpha=alpha),
        grid=(T // tm,),
        in_specs=[pl.BlockSpec((tm, att.shape[1]), row), pl.BlockSpec((tm, CONV_CH), row),
                  pl.BlockSpec((SB_HEADS, tm, SB_HEAD_DIM), lambda i: (0, i, 0)),
                  pl.BlockSpec((tm, D), row),
                  pl.BlockSpec((1, 1, D), per_b), pl.BlockSpec((1, 1, D), per_b),
                  pl.BlockSpec((1, 1, D), per_b),
                  pl.BlockSpec((1, D), const2), pl.BlockSpec((1, D), const2),
                  pl.BlockSpec(lw["w_o"].shape, const2), pl.BlockSpec(lw["wq"].shape, const2),
                  pl.BlockSpec(lw["k1"].shape, const3), pl.BlockSpec(lw["k2"].shape, const3)],
        out_specs=[pl.BlockSpec((tm, D), row), pl.BlockSpec((tm, D), row), route, route, route, route],
        out_shape=[jax.ShapeDtypeStruct((T, D), F32), jax.ShapeDtypeStruct((T, D), BF16),
                   route_shape, route_shape, route_shape, route_shape],
        compiler_params=_params("arbitrary"),
        name="mid",
    )(att, conv, sb, x, g1, sh2, sc2, lw["ln1_g"], lw["ln1_b"], lw["w_o"], lw["wq"],
      lw["k1"], lw["k2"])


def _peer_body(h2_ref, u_ref, vt_ref, cnt_ref, e1_ref, rank_ref, e2_ref, x1_ref, g2_ref, lng_ref,
               lnb_ref, o_ref, a_ref, w_ref, acc_ref, *, tm, rows, alpha):
    j = pl.program_id(1)

    @pl.when(j == 0)
    def _():
        acc_ref[...] = jnp.zeros_like(acc_ref)

    a_ref[...] = lax.dot_general(u_ref[...], h2_ref[...], _NT, preferred_element_type=F32)

    for rr in range(rows):
        rs = slice(rr * PEER_N_KEYS, (rr + 1) * PEER_N_KEYS)
        tw = 2 * LANES
        for ts in range(tm // tw):
            ln = slice(ts * tw, (ts + 1) * tw)
            g = jnp.zeros((PEER_N_KEYS, tw), BF16)
            for h in range(PEER_HEADS):
                cnt = cnt_ref[h, rr:rr + 1, ln]
                e1 = e1_ref[h, rr:rr + 1, ln]
                e2 = e2_ref[h, :, ln]
                g = g + jnp.where(rank_ref[h, :, ln] < cnt, e2, jnp.zeros_like(e2)) * e1
            a = a_ref[rs, ln]
            act = a * (1.0 + lax.erf(a * SQRT_HALF))
            w_ref[rs, ln] = act.astype(BF16) * g
    acc_ref[...] += jnp.dot(vt_ref[...], w_ref[...], preferred_element_type=F32)

    @pl.when(j == pl.num_programs(1) - 1)
    def _():
        ffn = acc_ref[...].T
        y = alpha * x1_ref[...] + (1.0 + g2_ref[0]) * ffn
        o_ref[...] = _ln_noaffine(y) * lng_ref[...] + lnb_ref[...]


def _peer(h2, cnt, e1, rank, e2, x1, g2, lw, B, S, tm, rows, alpha):
    T, D = x1.shape
    nt = S // tm
    ne = rows * PEER_N_KEYS
    n_steps = PEER_N_KEYS // rows
    row = lambda i, j: (i, 0)
    const2 = lambda i, j: (0, 0)
    return pl.pallas_call(
        functools.partial(_peer_body, tm=tm, rows=rows, alpha=alpha),
        grid=(T // tm, n_steps),
        in_specs=[pl.BlockSpec((tm, D), row),
                  pl.BlockSpec((ne, D), lambda i, j: (j, 0)),
                  pl.BlockSpec((D, ne), lambda i, j: (0, j)),
                  pl.BlockSpec((PEER_HEADS, rows, tm), lambda i, j: (0, j, i)),
                  pl.BlockSpec((PEER_HEADS, rows, tm), lambda i, j: (0, j, i)),
                  pl.BlockSpec((PEER_HEADS, PEER_N_KEYS, tm), lambda i, j: (0, 0, i)),
                  pl.BlockSpec((PEER_HEADS, PEER_N_KEYS, tm), lambda i, j: (0, 0, i)),
                  pl.BlockSpec((tm, D), row),
                  pl.BlockSpec((1, 1, D), lambda i, j: (i // nt, 0, 0)),
                  pl.BlockSpec((1, D), const2), pl.BlockSpec((1, D), const2)],
        out_specs=pl.BlockSpec((tm, D), row),
        out_shape=jax.ShapeDtypeStruct((T, D), F32),
        scratch_shapes=[pltpu.VMEM((ne, tm), F32), pltpu.VMEM((ne, tm), BF16),
                        pltpu.VMEM((D, tm), F32)],
        compiler_params=_params("arbitrary", "arbitrary"),
        name="peer",
    )(h2, lw["u"], lw["vt"], cnt, e1, rank, e2, x1, g2, lw["ln2_g"], lw["ln2_b"])


def _layer_weights(l, w_in, q_norm_g, kv_norm_g, w_uq, w_ukv, conv_w, w_o, ln1_g, ln1_b,
                   peer_wq, peer_k1, peer_k2, peer_u, peer_v, ln2_g, ln2_b):
    D = D_MODEL
    wi = w_in[l]
    pad = jnp.zeros((D, LANES - MLA_ROPE), F32)
    w_in_p = jnp.concatenate([wi[:, :416], pad, wi[:, 416:]], axis=1).astype(BF16)
    qk = MLA_NOPE + MLA_ROPE
    w_uq_p = jnp.pad(w_uq[l].reshape(MLA_Q_RANK, MLA_HEADS, qk),
                     ((0, 0), (0, 0), (0, HEAD_PAD - qk))).reshape(MLA_Q_RANK, -1).astype(BF16)
    ukv = w_ukv[l].reshape(MLA_KV_RANK, MLA_HEADS, MLA_NOPE + MLA_V)
    wk = jnp.pad(ukv[:, :, :MLA_NOPE], ((0, 0), (0, 0), (0, HEAD_PAD - MLA_NOPE)))
    expand = np.zeros((LANES, MLA_HEADS, HEAD_PAD), np.float32)
    for r in range(MLA_ROPE):
        expand[r, :, MLA_NOPE + r] = 1.0
    wk2 = jnp.concatenate([wk, jnp.asarray(expand)], axis=0).reshape(MLA_KV_RANK + LANES, -1).astype(BF16)
    wv = ukv[:, :, MLA_NOPE:].reshape(MLA_KV_RANK, -1).astype(BF16)
    half = MLA_ROPE // 2
    inv_freq = ROPE_THETA ** (-jnp.arange(half, dtype=F32) / half)
    invf = jnp.zeros((LANES,), F32).at[MLA_NOPE:MLA_NOPE + half].set(inv_freq)
    invf = invf.at[MLA_NOPE + half:MLA_NOPE + MLA_ROPE].set(inv_freq).reshape(1, LANES)
    return dict(
        w_in=w_in_p, qg=q_norm_g[l].reshape(1, -1), kvg=kv_norm_g[l].reshape(1, -1),
        w_uq=w_uq_p, wk2=wk2, wv=wv, conv_w=conv_w[l], invf=invf,
        w_o=w_o[l].astype(BF16), ln1_g=ln1_g[l].reshape(1, D), ln1_b=ln1_b[l].reshape(1, D),
        wq=peer_wq[l].astype(BF16), k1=peer_k1[l].astype(BF16), k2=peer_k2[l].astype(BF16),
        u=peer_u[l].astype(BF16), vt=peer_v[l].T.astype(BF16),
        ln2_g=ln2_g[l].reshape(1, D), ln2_b=ln2_b[l].reshape(1, D))


def _forward(x, c, positions, ada_w, ada_b, *weights, tm_in, tq_mla, t_sb, tm_mid, tm_peer, rows):
    B, S, D = x.shape
    T = B * S
    depth = ada_w.shape[0]
    alpha = (2.0 * depth) ** 0.25
    mod = _adaln(c, ada_w, ada_b)
    pos = positions.reshape(T, 1)
    xf = x.reshape(T, D)
    for l in range(depth):
        lw = _layer_weights(l, *weights)
        sh1, sc1, g1, sh2, sc2, g2 = [m.reshape(B, 1, D) for m in jnp.split(mod[l], 6, axis=-1)]
        q, k, v, conv, sbq, sbk, sbv = _mixer_in(xf, pos, sh1, sc1, lw, B, S, tm_in)
        att = _mla(q, k, v, B, S, tq_mla)
        sbo = _stickbreak(sbq, sbk, sbv, B, S, t_sb)
        x1, h2, cnt, e1, rank, e2 = _mid(att, conv, sbo, xf, g1, sh2, sc2, lw, B, S, tm_mid, alpha)
        xf = _peer(h2, cnt, e1, rank, e2, x1, g2, lw, B, S, tm_peer, rows, alpha)
    return xf.reshape(B, S, D)


def kernel(x, c, positions, ada_w, ada_b, w_in, q_norm_g, kv_norm_g, w_uq, w_ukv, conv_w, w_o,
           ln1_g, ln1_b, peer_wq, peer_k1, peer_k2, peer_u, peer_v, ln2_g, ln2_b):
    return _forward(x, c, positions, ada_w, ada_b, w_in, q_norm_g, kv_norm_g, w_uq, w_ukv, conv_w,
                    w_o, ln1_g, ln1_b, peer_wq, peer_k1, peer_k2, peer_u, peer_v, ln2_g, ln2_b,
                    tm_in=512, tq_mla=256, t_sb=256, tm_mid=256, tm_peer=512, rows=16)
```

```python
import functools
import math

import numpy as np
import jax
import jax.numpy as jnp
from jax import lax
from jax.experimental import pallas as pl
from jax.experimental.pallas import tpu as pltpu

F32 = jnp.float32
BF16 = jnp.bfloat16

D_MODEL = 1024
CHUNK = 64

MLA_HEADS = 8
MLA_NOPE = 64
MLA_ROPE = 32
MLA_V = 64
MLA_Q_RANK = 256
MLA_KV_RANK = 128
ROPE_THETA = 10000.0
HEAD_PAD = 128

CONV_CH = 256
CONV_WIDTH = 3
SB_HEADS = 4
SB_HEAD_DIM = 64
SB_WIDTH = 256

PEER_HEADS = 8
PEER_N_KEYS = 128
PEER_HALF = 128
PEER_TOPK = 16

LN_EPS = 1e-5
RMS_EPS = 1e-6
NEG_INF = -1e30
SQRT_HALF = 0.7071067811865476

LANES = 128
SUBLANES = 8
VMEM_LIMIT = 56 * 1024 * 1024

_O_QLAT, _O_KVLAT, _O_KR = 0, 256, 384
_O_CB, _O_CC, _O_CH = 512, 768, 1024
_O_SQ, _O_SK, _O_SV = 1280, 1536, 1792
IN_PAD = 2048

_NT = (((1,), (1,)), ((), ()))


def _ln_noaffine(x):
    mu = jnp.mean(x, axis=-1, keepdims=True)
    xc = x - mu
    var = jnp.mean(xc * xc, axis=-1, keepdims=True)
    return xc * lax.rsqrt(var + LN_EPS)


def _rms(x, gain):
    return x * lax.rsqrt(jnp.mean(x * x, axis=-1, keepdims=True) + RMS_EPS) * gain


def _params(*sem):
    return pltpu.CompilerParams(dimension_semantics=sem, vmem_limit_bytes=VMEM_LIMIT)


def _adaln_body(c_ref, w_ref, b_ref, o_ref):
    c = c_ref[...]
    ca = c * jax.nn.sigmoid(c)
    o_ref[0] = jnp.dot(ca, w_ref[0], preferred_element_type=F32,
                       precision=lax.Precision.HIGHEST) + b_ref[0]


def _adaln(c, ada_w, ada_b):
    L, D, N = ada_w.shape
    B = c.shape[0]
    tn = 1024
    return pl.pallas_call(
        _adaln_body,
        grid=(L, N // tn),
        in_specs=[pl.BlockSpec((B, D), lambda l, n: (0, 0)),
                  pl.BlockSpec((1, D, tn), lambda l, n: (l, 0, n)),
                  pl.BlockSpec((1, 1, tn), lambda l, n: (l, 0, n))],
        out_specs=pl.BlockSpec((1, B, tn), lambda l, n: (l, 0, n)),
        out_shape=jax.ShapeDtypeStruct((L, B, N), F32),
        compiler_params=_params("arbitrary", "arbitrary"),
        name="adaln",
    )(c, ada_w, ada_b.reshape(L, 1, N))


def _mixer_in_body(pos_ref, x_ref, sh_ref, sc_ref, win_ref, qg_ref, kvg_ref, wuq_ref, wk2_ref,
                   wv_ref, cw_ref, invf_ref,
                   q_ref, k_ref, v_ref, conv_ref, sbq_ref, sbk_ref, sbv_ref, ubuf_ref, *, tm):
    j = pl.program_id(1)
    h = _ln_noaffine(x_ref[...]) * (1.0 + sc_ref[0]) + sh_ref[0]
    proj = jnp.dot(h.astype(BF16), win_ref[...], preferred_element_type=F32)

    qn = _rms(proj[:, _O_QLAT:_O_QLAT + MLA_Q_RANK], qg_ref[...])
    kvn = _rms(proj[:, _O_KVLAT:_O_KVLAT + MLA_KV_RANK], kvg_ref[...])
    kr = proj[:, _O_KR:_O_KR + LANES]
    q = jnp.dot(qn.astype(BF16), wuq_ref[...], preferred_element_type=F32)
    kcat = jnp.concatenate([kvn, kr], axis=-1).astype(BF16)
    k = jnp.dot(kcat, wk2_ref[...], preferred_element_type=F32)
    v = jnp.dot(kvn.astype(BF16), wv_ref[...], preferred_element_type=F32)
    v_ref[...] = v.astype(BF16)

    ang = pos_ref[...].astype(F32) * invf_ref[...]
    cos = jnp.cos(ang)
    sin = jnp.sin(ang)
    lane = lax.broadcasted_iota(jnp.int32, (1, LANES), 1)
    half = MLA_ROPE // 2
    first = (lane >= MLA_NOPE) & (lane < MLA_NOPE + half)
    second = (lane >= MLA_NOPE + half) & (lane < MLA_NOPE + MLA_ROPE)
    s_prev = jnp.where(second, sin, 0.0)
    s_next = jnp.where(first, -sin, 0.0)
    qscale = (MLA_NOPE + MLA_ROPE) ** -0.5
    for hb in range(MLA_HEADS):
        sl = slice(hb * HEAD_PAD, (hb + 1) * HEAD_PAD)
        for src, dst, scale in ((q, q_ref, qscale), (k, k_ref, None)):
            xb = src[:, sl]
            y = (xb * cos + pltpu.roll(xb, half, 1) * s_prev
                 + pltpu.roll(xb, LANES - half, 1) * s_next)
            if scale is not None:
                y = y * scale
            dst[:, sl] = y.astype(BF16)

    cb = proj[:, _O_CB:_O_CB + CONV_CH]
    u = proj[:, _O_CC:_O_CC + CONV_CH] * proj[:, _O_CH:_O_CH + CONV_CH]

    @pl.when(j == 0)
    def _():
        ubuf_ref[0:SUBLANES, :] = jnp.zeros((SUBLANES, CONV_CH), F32)

    ubuf_ref[SUBLANES:SUBLANES + tm, :] = u
    y = (cw_ref[0:1, :] * ubuf_ref[SUBLANES - 2:SUBLANES - 2 + tm, :]
         + cw_ref[1:2, :] * ubuf_ref[SUBLANES - 1:SUBLANES - 1 + tm, :]
         + cw_ref[2:3, :] * u)
    conv_ref[...] = (cb * y).astype(BF16)
    ubuf_ref[0:SUBLANES, :] = ubuf_ref[tm:tm + SUBLANES, :]

    for hh in range(SB_HEADS):
        o = hh * SB_HEAD_DIM
        sbq_ref[hh] = (proj[:, _O_SQ + o:_O_SQ + o + SB_HEAD_DIM] * (SB_HEAD_DIM ** -0.5)).astype(BF16)
        sbk_ref[hh] = proj[:, _O_SK + o:_O_SK + o + SB_HEAD_DIM].astype(BF16)
        sbv_ref[hh] = proj[:, _O_SV + o:_O_SV + o + SB_HEAD_DIM].astype(BF16)


def _mixer_in(x, pos, sh, sc, lw, B, S, tm):
    T, D = x.shape
    nt = S // tm
    row = lambda b, j: (b * nt + j, 0)
    per_b = lambda b, j: (b, 0, 0)
    const = lambda b, j: (0, 0)
    full = lambda a: pl.BlockSpec(a.shape, const)
    outs = [(MLA_HEADS * HEAD_PAD, BF16), (MLA_HEADS * HEAD_PAD, BF16), (MLA_HEADS * MLA_V, BF16),
            (CONV_CH, BF16)]
    sb_spec = pl.BlockSpec((SB_HEADS, tm, SB_HEAD_DIM), lambda b, j: (0, b * nt + j, 0))
    sb_shape = jax.ShapeDtypeStruct((SB_HEADS, T, SB_HEAD_DIM), BF16)
    return pl.pallas_call(
        functools.partial(_mixer_in_body, tm=tm),
        grid=(B, nt),
        in_specs=[pl.BlockSpec((tm, 1), row), pl.BlockSpec((tm, D), row),
                  pl.BlockSpec((1, 1, D), per_b), pl.BlockSpec((1, 1, D), per_b),
                  full(lw["w_in"]), full(lw["qg"]), full(lw["kvg"]), full(lw["w_uq"]),
                  full(lw["wk2"]), full(lw["wv"]), full(lw["conv_w"]), full(lw["invf"])],
        out_specs=[pl.BlockSpec((tm, n), row) for n, _ in outs] + [sb_spec] * 3,
        out_shape=[jax.ShapeDtypeStruct((T, n), dt) for n, dt in outs] + [sb_shape] * 3,
        scratch_shapes=[pltpu.VMEM((tm + SUBLANES, CONV_CH), F32)],
        compiler_params=_params("arbitrary", "arbitrary"),
        name="mixer_in",
    )(pos, x, sh, sc, lw["w_in"], lw["qg"], lw["kvg"], lw["w_uq"], lw["wk2"], lw["wv"],
      lw["conv_w"], lw["invf"])


def _mla_body(q_ref, k_ref, v_ref, o_ref, *, S, tq):
    r = lax.broadcasted_iota(jnp.int32, (tq, tq), 0) // CHUNK
    c = lax.broadcasted_iota(jnp.int32, (tq, tq), 1) // CHUNK
    visible = c <= r

    for hh in range(2):
        hs = slice(hh * HEAD_PAD, (hh + 1) * HEAD_PAD)
        vs = slice(hh * MLA_V, (hh + 1) * MLA_V)
        for i in range(S // tq):
            rows = slice(i * tq, (i + 1) * tq)
            q = q_ref[rows, hs]
            s_d = lax.dot_general(q, k_ref[rows, hs], _NT, preferred_element_type=F32)
            s_d = jnp.where(visible, s_d, NEG_INF)
            m = jnp.max(s_d, axis=-1, keepdims=True)
            if i > 0:
                s_o = lax.dot_general(q, k_ref[0:i * tq, hs], _NT, preferred_element_type=F32)
                m = jnp.maximum(m, jnp.max(s_o, axis=-1, keepdims=True))
            p_d = jnp.exp(s_d - m)
            l = jnp.sum(p_d, axis=-1, keepdims=True)
            acc = jnp.dot(p_d.astype(BF16), v_ref[rows, vs], preferred_element_type=F32)
            if i > 0:
                p_o = jnp.exp(s_o - m)
                l = l + jnp.sum(p_o, axis=-1, keepdims=True)
                acc = acc + jnp.dot(p_o.astype(BF16), v_ref[0:i * tq, vs], preferred_element_type=F32)
            o_ref[rows, vs] = (acc / l).astype(BF16)


def _mla(q, k, v, B, S, tq):
    T = q.shape[0]
    hp = MLA_HEADS // 2
    return pl.pallas_call(
        functools.partial(_mla_body, S=S, tq=tq),
        grid=(B, hp),
        in_specs=[pl.BlockSpec((S, 2 * HEAD_PAD), lambda b, h: (b, h)),
                  pl.BlockSpec((S, 2 * HEAD_PAD), lambda b, h: (b, h)),
                  pl.BlockSpec((S, 2 * MLA_V), lambda b, h: (b, h))],
        out_specs=pl.BlockSpec((S, 2 * MLA_V), lambda b, h: (b, h)),
        out_shape=jax.ShapeDtypeStruct((T, MLA_HEADS * MLA_V), BF16),
        compiler_params=_params("arbitrary", "arbitrary"),
        name="mla",
    )(q, k, v)


def _softplus(z):
    return jnp.maximum(z, 0.0) + jnp.log1p(jnp.exp(-jnp.abs(z)))


def _stickbreak_body(q_ref, k_ref, v_ref, o_ref, *, S, t):
    ri = lax.broadcasted_iota(jnp.int32, (t, t), 0)
    ci = lax.broadcasted_iota(jnp.int32, (t, t), 1)
    strict = ci < ri
    upper = jnp.where(ri >= ci, 1.0, 0.0).astype(BF16)

    for i in range(S // t):
        rows = slice(i * t, (i + 1) * t)
        z = lax.dot_general(q_ref[0, rows, :], k_ref[0, 0:(i + 1) * t, :], _NT,
                            preferred_element_type=F32)
        sp = _softplus(z)
        after = jnp.zeros((t, 1), F32)
        ws = [None] * (i + 1)
        for j in range(i, -1, -1):
            cols = slice(j * t, (j + 1) * t)
            sp_j = sp[:, cols]
            if j == i:
                sp_j = jnp.where(strict, sp_j, 0.0)
            hi = sp_j.astype(BF16)
            lo = (sp_j - hi.astype(F32)).astype(BF16)
            incl = (jnp.dot(hi, upper, preferred_element_type=F32)
                    + jnp.dot(lo, upper, preferred_element_type=F32))
            w = jnp.exp(z[:, cols] - incl - after)
            if j == i:
                w = jnp.where(strict, w, 0.0)
            ws[j] = w.astype(BF16)
            after = after + incl[:, 0:1]
        w_all = ws[0] if i == 0 else jnp.concatenate(ws, axis=1)
        o = jnp.dot(w_all, v_ref[0, 0:(i + 1) * t, :], preferred_element_type=F32)
        o_ref[0, rows, :] = o.astype(BF16)


def _stickbreak(q, k, v, B, S, t):
    T = q.shape[1]
    spec = pl.BlockSpec((1, S, SB_HEAD_DIM), lambda b, h: (h, b, 0))
    return pl.pallas_call(
        functools.partial(_stickbreak_body, S=S, t=t),
        grid=(B, SB_HEADS),
        in_specs=[spec, spec, spec],
        out_specs=spec,
        out_shape=jax.ShapeDtypeStruct((SB_HEADS, T, SB_HEAD_DIM), BF16),
        compiler_params=_params("arbitrary", "arbitrary"),
        name="stickbreak",
    )(q, k, v)


def _oddeven_pairs(n):
    pairs = []
    p = 1
    while p < n:
        k = p
        while k >= 1:
            for j in range(k % p, n - k, 2 * k):
                for i in range(min(k, n - j - k)):
                    if (i + j) // (2 * p) == (i + j + k) // (2 * p):
                        pairs.append((i + j, i + j + k))
            k //= 2
        p *= 2
    return pairs


_SORT16 = _oddeven_pairs(PEER_TOPK)


def _cmpx(xs, i, j):
    a, b = xs[i], xs[j]
    if a is None:
        xs[i], xs[j] = b, None
    elif b is not None:
        xs[i], xs[j] = jnp.maximum(a, b), jnp.minimum(a, b)


def _bitonic_merge(xs):
    n = len(xs)
    d = n // 2
    while d >= 1:
        for i in range(n):
            if (i // d) % 2 == 0:
                _cmpx(xs, i, i + d)
        d //= 2


def _top16_sorted(xs):
    xs = list(xs)
    for i, j in _SORT16:
        _cmpx(xs, i, j)
    for shift in (4, 2, 1):
        ps = [None if x is None else pltpu.roll(x, shift, 0) for x in xs]
        merged = []
        for i in range(PEER_TOPK):
            a, b = xs[i], ps[PEER_TOPK - 1 - i]
            merged.append(b if a is None else a if b is None else jnp.maximum(a, b))
        xs = merged
        _bitonic_merge(xs)
    return xs


def _top16_pair_sums(v1, v2):
    sub = lax.broadcasted_iota(jnp.int32, v1[0].shape, 0)

    def pack(vals):
        out = vals[0]
        for s in range(1, SUBLANES):
            out = jnp.where(sub == s, vals[s], out)
        return out

    b_lo, b_hi = pack(v2[:SUBLANES]), pack(v2[SUBLANES:])
    cands = [v1[0] + b_lo, v1[0] + b_hi]
    for a in range(1, SUBLANES):
        s = v1[a] + b_lo
        lim = PEER_TOPK // (a + 1)
        cands.append(s if lim >= SUBLANES else jnp.where(sub < lim, s, -jnp.inf))
    cands.append(pack(v1[SUBLANES:]) + v2[0])
    cands += [None] * (PEER_TOPK - len(cands))
    return _top16_sorted(cands)


def _count_prefix(pred, vals):
    n = len(vals)
    bits = []

    def pivot(level_vals, k):
        if k == len(bits):
            return level_vals[0]
        half = len(level_vals) // 2
        return jnp.where(bits[k], pivot(level_vals[half:], k + 1), pivot(level_vals[:half], k + 1))

    step = n // 2
    while step >= 1:
        cands = [vals[base + step - 1] for base in range(0, n, 2 * step)]
        bits.append(pred(pivot(cands, 0)))
        step //= 2
    count = jnp.where(bits[0], float(n // 2), 0.0)
    for k, b in enumerate(bits[1:], start=1):
        count = count + jnp.where(b, float(n >> (k + 1)), 0.0)
    return jnp.where(pred(vals[n - 1]), float(n), count)


def _mid_body(att_ref, conv_ref, sb_ref, x_ref, g1_ref, sh2_ref, sc2_ref, lng_ref, lnb_ref,
              wo_ref, wq_ref, k1_ref, k2_ref,
              x1_ref, h2_ref, cnt_ref, e1_ref, rank_ref, e2_ref, *, alpha):
    cat = jnp.concatenate([att_ref[...], conv_ref[...]] + [sb_ref[h] for h in range(SB_HEADS)], axis=-1)
    mix = jnp.dot(cat, wo_ref[...], preferred_element_type=F32)
    x1 = _ln_noaffine(alpha * x_ref[...] + (1.0 + g1_ref[0]) * mix) * lng_ref[...] + lnb_ref[...]
    x1_ref[...] = x1
    h2 = (_ln_noaffine(x1) * (1.0 + sc2_ref[0]) + sh2_ref[0]).astype(BF16)
    h2_ref[...] = h2
    qp = jnp.dot(h2, wq_ref[...], preferred_element_type=F32).astype(BF16)
    _route(qp, k1_ref, k2_ref, cnt_ref, e1_ref, rank_ref, e2_ref)


def _route(qp, k1_ref, k2_ref, cnt_ref, e1_ref, rank_ref, e2_ref):
    for h in range(PEER_HEADS):
        o = h * 2 * PEER_HALF
        s1 = lax.dot_general(k1_ref[h], qp[:, o:o + PEER_HALF], _NT, preferred_element_type=F32)
        s2 = lax.dot_general(k2_ref[h], qp[:, o + PEER_HALF:o + 2 * PEER_HALF], _NT,
                             preferred_element_type=F32)
        split = lambda s: [s[SUBLANES * i:SUBLANES * (i + 1), :] for i in range(PEER_N_KEYS // SUBLANES)]
        v1 = _top16_sorted(split(s1))
        v2 = _top16_sorted(split(s2))
        top = _top16_pair_sums(v1, v2)
        tau = top[PEER_TOPK - 1][0:1, :]
        z = jnp.ones_like(top[0])
        for t in top[1:]:
            z = z + jnp.exp(t - top[0])
        e1_ref[h] = jnp.exp(s1 - v1[0][0:1, :]).astype(BF16)
        e2_ref[h] = (jnp.exp(s2 - v2[0][0:1, :]) * (0.5 / z[0:1, :])).astype(BF16)
        col = [v[0:1, :] for v in v2]
        cnt_ref[h] = _count_prefix(lambda vb: s1 + vb >= tau, col).astype(BF16)
        rank_ref[h] = _count_prefix(lambda vb: s2 < vb, col).astype(BF16)


def _mid(att, conv, sb, x, g1, sh2, sc2, lw, B, S, tm, alpha):
    T, D = x.shape
    nt = S // tm
    row = lambda i: (i, 0)
    per_b = lambda i: (i // nt, 0, 0)
    const2 = lambda i: (0, 0)
    const3 = lambda i: (0, 0, 0)
    route = pl.BlockSpec((PEER_HEADS, PEER_N_KEYS, tm), lambda i: (0, 0, i))
    route_shape = jax.ShapeDtypeStruct((PEER_HEADS, PEER_N_KEYS, T), BF16)
    return pl.pallas_call(
        functools.partial(_mid_body, alpha=alpha),
        grid=(T // tm,),
        in_specs=[pl.BlockSpec((tm, att.shape[1]), row), pl.BlockSpec((tm, CONV_CH), row),
                  pl.BlockSpec((SB_HEADS, tm, SB_HEAD_DIM), lambda i: (0, i, 0)),
                  pl.BlockSpec((tm, D), row),
                  pl.BlockSpec((1, 1, D), per_b), pl.BlockSpec((1, 1, D), per_b),
                  pl.BlockSpec((1, 1, D), per_b),
                  pl.BlockSpec((1, D), const2), pl.BlockSpec((1, D), const2),
                  pl.BlockSpec(lw["w_o"].shape, const2), pl.BlockSpec(lw["wq"].shape, const2),
                  pl.BlockSpec(lw["k1"].shape, const3), pl.BlockSpec(lw["k2"].shape, const3)],
        out_specs=[pl.BlockSpec((tm, D), row), pl.BlockSpec((tm, D), row), route, route, route, route],
        out_shape=[jax.ShapeDtypeStruct((T, D), F32), jax.ShapeDtypeStruct((T, D), BF16),
                   route_shape, route_shape, route_shape, route_shape],
        compiler_params=_params("arbitrary"),
        name="mid",
    )(att, conv, sb, x, g1, sh2, sc2, lw["ln1_g"], lw["ln1_b"], lw["w_o"], lw["wq"],
      lw["k1"], lw["k2"])


def _peer_body(h2_ref, u_ref, vt_ref, cnt_ref, e1_ref, rank_ref, e2_ref, x1_ref, g2_ref, lng_ref,
               lnb_ref, o_ref, a_ref, w_ref, acc_ref, *, tm, rows, alpha):
    j = pl.program_id(1)

    @pl.when(j == 0)
    def _():
        acc_ref[...] = jnp.zeros_like(acc_ref)

    for e0 in range(0, rows * PEER_N_KEYS, 1024):
        for t0 in range(0, tm, 2 * LANES):
            a_ref[e0:e0 + 1024, t0:t0 + 2 * LANES] = lax.dot_general(
                u_ref[e0:e0 + 1024, :], h2_ref[t0:t0 + 2 * LANES, :], _NT,
                preferred_element_type=F32)

    for rr in range(rows):
        rs = slice(rr * PEER_N_KEYS, (rr + 1) * PEER_N_KEYS)
        tw = 2 * LANES
        for ts in range(tm // tw):
            ln = slice(ts * tw, (ts + 1) * tw)
            g = jnp.zeros((PEER_N_KEYS, tw), BF16)
            for h in range(PEER_HEADS):
                cnt = cnt_ref[h, rr:rr + 1, ln]
                e1 = e1_ref[h, rr:rr + 1, ln]
                e2 = e2_ref[h, :, ln]
                g = g + jnp.where(rank_ref[h, :, ln] < cnt, e2, jnp.zeros_like(e2)) * e1
            a = a_ref[rs, ln].astype(BF16)
            act = a * (1.0 + lax.erf(a * SQRT_HALF))
            w_ref[rs, ln] = act * g
    acc_ref[...] += jnp.dot(vt_ref[...], w_ref[...], preferred_element_type=F32)

    @pl.when(j == pl.num_programs(1) - 1)
    def _():
        ffn = acc_ref[...].T
        y = alpha * x1_ref[...] + (1.0 + g2_ref[0]) * ffn
        o_ref[...] = _ln_noaffine(y) * lng_ref[...] + lnb_ref[...]


def _peer(h2, cnt, e1, rank, e2, x1, g2, lw, B, S, tm, rows, alpha):
    T, D = x1.shape
    nt = S // tm
    ne = rows * PEER_N_KEYS
    n_steps = PEER_N_KEYS // rows
    row = lambda i, j: (i, 0)
    const2 = lambda i, j: (0, 0)
    return pl.pallas_call(
        functools.partial(_peer_body, tm=tm, rows=rows, alpha=alpha),
        grid=(T // tm, n_steps),
        in_specs=[pl.BlockSpec((tm, D), row),
                  pl.BlockSpec((ne, D), lambda i, j: (j, 0)),
                  pl.BlockSpec((D, ne), lambda i, j: (0, j)),
                  pl.BlockSpec((PEER_HEADS, rows, tm), lambda i, j: (0, j, i)),
                  pl.BlockSpec((PEER_HEADS, rows, tm), lambda i, j: (0, j, i)),
                  pl.BlockSpec((PEER_HEADS, PEER_N_KEYS, tm), lambda i, j: (0, 0, i)),
                  pl.BlockSpec((PEER_HEADS, PEER_N_KEYS, tm), lambda i, j: (0, 0, i)),
                  pl.BlockSpec((tm, D), row),
                  pl.BlockSpec((1, 1, D), lambda i, j: (i // nt, 0, 0)),
                  pl.BlockSpec((1, D), const2), pl.BlockSpec((1, D), const2)],
        out_specs=pl.BlockSpec((tm, D), row),
        out_shape=jax.ShapeDtypeStruct((T, D), F32),
        scratch_shapes=[pltpu.VMEM((ne, tm), F32), pltpu.VMEM((ne, tm), BF16),
                        pltpu.VMEM((D, tm), F32)],
        compiler_params=_params("arbitrary", "arbitrary"),
        name="peer",
    )(h2, lw["u"], lw["vt"], cnt, e1, rank, e2, x1, g2, lw["ln2_g"], lw["ln2_b"])


def _layer_weights(l, w_in, q_norm_g, kv_norm_g, w_uq, w_ukv, conv_w, w_o, ln1_g, ln1_b,
                   peer_wq, peer_k1, peer_k2, peer_u, peer_v, ln2_g, ln2_b):
    D = D_MODEL
    wi = w_in[l]
    pad = jnp.zeros((D, LANES - MLA_ROPE), F32)
    w_in_p = jnp.concatenate([wi[:, :416], pad, wi[:, 416:]], axis=1).astype(BF16)
    qk = MLA_NOPE + MLA_ROPE
    w_uq_p = jnp.pad(w_uq[l].reshape(MLA_Q_RANK, MLA_HEADS, qk),
                     ((0, 0), (0, 0), (0, HEAD_PAD - qk))).reshape(MLA_Q_RANK, -1).astype(BF16)
    ukv = w_ukv[l].reshape(MLA_KV_RANK, MLA_HEADS, MLA_NOPE + MLA_V)
    wk = jnp.pad(ukv[:, :, :MLA_NOPE], ((0, 0), (0, 0), (0, HEAD_PAD - MLA_NOPE)))
    expand = np.zeros((LANES, MLA_HEADS, HEAD_PAD), np.float32)
    for r in range(MLA_ROPE):
        expand[r, :, MLA_NOPE + r] = 1.0
    wk2 = jnp.concatenate([wk, jnp.asarray(expand)], axis=0).reshape(MLA_KV_RANK + LANES, -1).astype(BF16)
    wv = ukv[:, :, MLA_NOPE:].reshape(MLA_KV_RANK, -1).astype(BF16)
    half = MLA_ROPE // 2
    inv_freq = ROPE_THETA ** (-jnp.arange(half, dtype=F32) / half)
    invf = jnp.zeros((LANES,), F32).at[MLA_NOPE:MLA_NOPE + half].set(inv_freq)
    invf = invf.at[MLA_NOPE + half:MLA_NOPE + MLA_ROPE].set(inv_freq).reshape(1, LANES)
    return dict(
        w_in=w_in_p, qg=q_norm_g[l].reshape(1, -1), kvg=kv_norm_g[l].reshape(1, -1),
        w_uq=w_uq_p, wk2=wk2, wv=wv, conv_w=conv_w[l], invf=invf,
        w_o=w_o[l].astype(BF16), ln1_g=ln1_g[l].reshape(1, D), ln1_b=ln1_b[l].reshape(1, D),
        wq=peer_wq[l].astype(BF16), k1=peer_k1[l].astype(BF16), k2=peer_k2[l].astype(BF16),
        u=peer_u[l].astype(BF16), vt=peer_v[l].T.astype(BF16),
        ln2_g=ln2_g[l].reshape(1, D), ln2_b=ln2_b[l].reshape(1, D))


def _forward(x, c, positions, ada_w, ada_b, *weights, tm_in, tq_mla, t_sb, tm_mid, tm_peer, rows):
    B, S, D = x.shape
    T = B * S
    depth = ada_w.shape[0]
    alpha = (2.0 * depth) ** 0.25
    mod = _adaln(c, ada_w, ada_b)
    pos = positions.reshape(T, 1)
    xf = x.reshape(T, D)
    for l in range(depth):
        lw = _layer_weights(l, *weights)
        sh1, sc1, g1, sh2, sc2, g2 = [m.reshape(B, 1, D) for m in jnp.split(mod[l], 6, axis=-1)]
        q, k, v, conv, sbq, sbk, sbv = _mixer_in(xf, pos, sh1, sc1, lw, B, S, tm_in)
        att = _mla(q, k, v, B, S, tq_mla)
        sbo = _stickbreak(sbq, sbk, sbv, B, S, t_sb)
        x1, h2, cnt, e1, rank, e2 = _mid(att, conv, sbo, xf, g1, sh2, sc2, lw, B, S, tm_mid, alpha)
        xf = _peer(h2, cnt, e1, rank, e2, x1, g2, lw, B, S, tm_peer, rows, alpha)
    return xf.reshape(B, S, D)


def kernel(x, c, positions, ada_w, ada_b, w_in, q_norm_g, kv_norm_g, w_uq, w_ukv, conv_w, w_o,
           ln1_g, ln1_b, peer_wq, peer_k1, peer_k2, peer_u, peer_v, ln2_g, ln2_b):
    return _forward(x, c, positions, ada_w, ada_b, w_in, q_norm_g, kv_norm_g, w_uq, w_ukv, conv_w,
                    w_o, ln1_g, ln1_b, peer_wq, peer_k1, peer_k2, peer_u, peer_v, ln2_g, ln2_b,
                    tm_in=512, tq_mla=256, t_sb=256, tm_mid=256, tm_peer=512, rows=16)
```

```python
import functools
import math

import numpy as np
import jax
import jax.numpy as jnp
from jax import lax
from jax.experimental import pallas as pl
from jax.experimental.pallas import tpu as pltpu

F32 = jnp.float32
BF16 = jnp.bfloat16

D_MODEL = 1024
CHUNK = 64

MLA_HEADS = 8
MLA_NOPE = 64
MLA_ROPE = 32
MLA_V = 64
MLA_Q_RANK = 256
MLA_KV_RANK = 128
ROPE_THETA = 10000.0
HEAD_PAD = 128

CONV_CH = 256
CONV_WIDTH = 3
SB_HEADS = 4
SB_HEAD_DIM = 64
SB_WIDTH = 256

PEER_HEADS = 8
PEER_N_KEYS = 128
PEER_HALF = 128
PEER_TOPK = 16

LN_EPS = 1e-5
RMS_EPS = 1e-6
NEG_INF = -1e30
SQRT_HALF = 0.7071067811865476

LANES = 128
SUBLANES = 8
VMEM_LIMIT = 56 * 1024 * 1024

_O_QLAT, _O_KVLAT, _O_KR = 0, 256, 384
_O_CB, _O_CC, _O_CH = 512, 768, 1024
_O_SQ, _O_SK, _O_SV = 1280, 1536, 1792
IN_PAD = 2048

_NT = (((1,), (1,)), ((), ()))


def _ln_noaffine(x):
    mu = jnp.mean(x, axis=-1, keepdims=True)
    xc = x - mu
    var = jnp.mean(xc * xc, axis=-1, keepdims=True)
    return xc * lax.rsqrt(var + LN_EPS)


def _rms(x, gain):
    return x * lax.rsqrt(jnp.mean(x * x, axis=-1, keepdims=True) + RMS_EPS) * gain


def _params(*sem):
    return pltpu.CompilerParams(dimension_semantics=sem, vmem_limit_bytes=VMEM_LIMIT)


def _adaln_body(c_ref, w_ref, b_ref, o_ref):
    c = c_ref[...]
    ca = c * jax.nn.sigmoid(c)
    o_ref[0] = jnp.dot(ca, w_ref[0], preferred_element_type=F32,
                       precision=lax.Precision.HIGHEST) + b_ref[0]


def _adaln(c, ada_w, ada_b):
    L, D, N = ada_w.shape
    B = c.shape[0]
    tn = 1024
    return pl.pallas_call(
        _adaln_body,
        grid=(L, N // tn),
        in_specs=[pl.BlockSpec((B, D), lambda l, n: (0, 0)),
                  pl.BlockSpec((1, D, tn), lambda l, n: (l, 0, n)),
                  pl.BlockSpec((1, 1, tn), lambda l, n: (l, 0, n))],
        out_specs=pl.BlockSpec((1, B, tn), lambda l, n: (l, 0, n)),
        out_shape=jax.ShapeDtypeStruct((L, B, N), F32),
        compiler_params=_params("arbitrary", "arbitrary"),
        name="adaln",
    )(c, ada_w, ada_b.reshape(L, 1, N))


def _rope_tables_body(pos_ref, invf_ref, cos_ref, sin_ref):
    ang = pos_ref[...].astype(F32) * invf_ref[...]
    cos_ref[...] = jnp.cos(ang)
    sin_ref[...] = jnp.sin(ang)


def _rope_tables(pos, invf, tm):
    T = pos.shape[0]
    row = lambda i: (i, 0)
    tab = jax.ShapeDtypeStruct((T, LANES), F32)
    return pl.pallas_call(
        _rope_tables_body,
        grid=(T // tm,),
        in_specs=[pl.BlockSpec((tm, 1), row), pl.BlockSpec((1, LANES), lambda i: (0, 0))],
        out_specs=[pl.BlockSpec((tm, LANES), row)] * 2,
        out_shape=[tab, tab],
        compiler_params=_params("arbitrary"),
        name="rope_tables",
    )(pos, invf)


def _mixer_in_body(cos_ref, sin_ref, x_ref, sh_ref, sc_ref, win_ref, qg_ref, kvg_ref, wuq_ref,
                   wk2_ref, wv_ref, cw_ref,
                   q_ref, k_ref, v_ref, conv_ref, sbq_ref, sbk_ref, sbv_ref, ubuf_ref, *, tm):
    j = pl.program_id(1)
    h = _ln_noaffine(x_ref[...]) * (1.0 + sc_ref[0]) + sh_ref[0]
    proj = jnp.dot(h.astype(BF16), win_ref[...], preferred_element_type=F32)

    qn = _rms(proj[:, _O_QLAT:_O_QLAT + MLA_Q_RANK], qg_ref[...])
    kvn = _rms(proj[:, _O_KVLAT:_O_KVLAT + MLA_KV_RANK], kvg_ref[...])
    kr = proj[:, _O_KR:_O_KR + LANES]
    q = jnp.dot(qn.astype(BF16), wuq_ref[...], preferred_element_type=F32)
    kcat = jnp.concatenate([kvn, kr], axis=-1).astype(BF16)
    k = jnp.dot(kcat, wk2_ref[...], preferred_element_type=F32)
    v = jnp.dot(kvn.astype(BF16), wv_ref[...], preferred_element_type=F32)
    v_ref[...] = v.astype(BF16)

    cos = cos_ref[...]
    sin = sin_ref[...]
    lane = lax.broadcasted_iota(jnp.int32, (1, LANES), 1)
    half = MLA_ROPE // 2
    first = (lane >= MLA_NOPE) & (lane < MLA_NOPE + half)
    second = (lane >= MLA_NOPE + half) & (lane < MLA_NOPE + MLA_ROPE)
    s_prev = jnp.where(second, sin, 0.0)
    s_next = jnp.where(first, -sin, 0.0)
    qscale = (MLA_NOPE + MLA_ROPE) ** -0.5
    for hb in range(MLA_HEADS):
        sl = slice(hb * HEAD_PAD, (hb + 1) * HEAD_PAD)
        for src, dst, scale in ((q, q_ref, qscale), (k, k_ref, None)):
            xb = src[:, sl]
            y = (xb * cos + pltpu.roll(xb, half, 1) * s_prev
                 + pltpu.roll(xb, LANES - half, 1) * s_next)
            if scale is not None:
                y = y * scale
            dst[:, sl] = y.astype(BF16)

    cb = proj[:, _O_CB:_O_CB + CONV_CH]
    u = proj[:, _O_CC:_O_CC + CONV_CH] * proj[:, _O_CH:_O_CH + CONV_CH]

    @pl.when(j == 0)
    def _():
        ubuf_ref[0:SUBLANES, :] = jnp.zeros((SUBLANES, CONV_CH), F32)

    ubuf_ref[SUBLANES:SUBLANES + tm, :] = u
    y = (cw_ref[0:1, :] * ubuf_ref[SUBLANES - 2:SUBLANES - 2 + tm, :]
         + cw_ref[1:2, :] * ubuf_ref[SUBLANES - 1:SUBLANES - 1 + tm, :]
         + cw_ref[2:3, :] * u)
    conv_ref[...] = (cb * y).astype(BF16)
    ubuf_ref[0:SUBLANES, :] = ubuf_ref[tm:tm + SUBLANES, :]

    for hh in range(SB_HEADS):
        o = hh * SB_HEAD_DIM
        sbq_ref[hh] = (proj[:, _O_SQ + o:_O_SQ + o + SB_HEAD_DIM] * (SB_HEAD_DIM ** -0.5)).astype(BF16)
        sbk_ref[hh] = proj[:, _O_SK + o:_O_SK + o + SB_HEAD_DIM].astype(BF16)
        sbv_ref[hh] = proj[:, _O_SV + o:_O_SV + o + SB_HEAD_DIM].astype(BF16)


def _mixer_in(x, cos, sin, sh, sc, lw, B, S, tm):
    T, D = x.shape
    nt = S // tm
    row = lambda b, j: (b * nt + j, 0)
    per_b = lambda b, j: (b, 0, 0)
    const = lambda b, j: (0, 0)
    full = lambda a: pl.BlockSpec(a.shape, const)
    outs = [(MLA_HEADS * HEAD_PAD, BF16), (MLA_HEADS * HEAD_PAD, BF16), (MLA_HEADS * MLA_V, BF16),
            (CONV_CH, BF16)]
    sb_spec = pl.BlockSpec((SB_HEADS, tm, SB_HEAD_DIM), lambda b, j: (0, b * nt + j, 0))
    sb_shape = jax.ShapeDtypeStruct((SB_HEADS, T, SB_HEAD_DIM), BF16)
    return pl.pallas_call(
        functools.partial(_mixer_in_body, tm=tm),
        grid=(B, nt),
        in_specs=[pl.BlockSpec((tm, LANES), row), pl.BlockSpec((tm, LANES), row),
                  pl.BlockSpec((tm, D), row),
                  pl.BlockSpec((1, 1, D), per_b), pl.BlockSpec((1, 1, D), per_b),
                  full(lw["w_in"]), full(lw["qg"]), full(lw["kvg"]), full(lw["w_uq"]),
                  full(lw["wk2"]), full(lw["wv"]), full(lw["conv_w"])],
        out_specs=[pl.BlockSpec((tm, n), row) for n, _ in outs] + [sb_spec] * 3,
        out_shape=[jax.ShapeDtypeStruct((T, n), dt) for n, dt in outs] + [sb_shape] * 3,
        scratch_shapes=[pltpu.VMEM((tm + SUBLANES, CONV_CH), F32)],
        compiler_params=_params("arbitrary", "arbitrary"),
        name="mixer_in",
    )(cos, sin, x, sh, sc, lw["w_in"], lw["qg"], lw["kvg"], lw["w_uq"], lw["wk2"], lw["wv"],
      lw["conv_w"])


def _mla_body(q_ref, k_ref, v_ref, o_ref, *, S, tq):
    r = lax.broadcasted_iota(jnp.int32, (tq, tq), 0) // CHUNK
    c = lax.broadcasted_iota(jnp.int32, (tq, tq), 1) // CHUNK
    visible = c <= r

    for hh in range(2):
        hs = slice(hh * HEAD_PAD, (hh + 1) * HEAD_PAD)
        vs = slice(hh * MLA_V, (hh + 1) * MLA_V)
        for i in range(S // tq):
            rows = slice(i * tq, (i + 1) * tq)
            q = q_ref[rows, hs]
            s_d = lax.dot_general(q, k_ref[rows, hs], _NT, preferred_element_type=F32)
            s_d = jnp.where(visible, s_d, NEG_INF)
            m = jnp.max(s_d, axis=-1, keepdims=True)
            if i > 0:
                s_o = lax.dot_general(q, k_ref[0:i * tq, hs], _NT, preferred_element_type=F32)
                m = jnp.maximum(m, jnp.max(s_o, axis=-1, keepdims=True))
            p_d = jnp.exp(s_d - m)
            l = jnp.sum(p_d, axis=-1, keepdims=True)
            acc = jnp.dot(p_d.astype(BF16), v_ref[rows, vs], preferred_element_type=F32)
            if i > 0:
                p_o = jnp.exp(s_o - m)
                l = l + jnp.sum(p_o, axis=-1, keepdims=True)
                acc = acc + jnp.dot(p_o.astype(BF16), v_ref[0:i * tq, vs], preferred_element_type=F32)
            o_ref[rows, vs] = (acc / l).astype(BF16)


def _mla(q, k, v, B, S, tq):
    T = q.shape[0]
    hp = MLA_HEADS // 2
    return pl.pallas_call(
        functools.partial(_mla_body, S=S, tq=tq),
        grid=(B, hp),
        in_specs=[pl.BlockSpec((S, 2 * HEAD_PAD), lambda b, h: (b, h)),
                  pl.BlockSpec((S, 2 * HEAD_PAD), lambda b, h: (b, h)),
                  pl.BlockSpec((S, 2 * MLA_V), lambda b, h: (b, h))],
        out_specs=pl.BlockSpec((S, 2 * MLA_V), lambda b, h: (b, h)),
        out_shape=jax.ShapeDtypeStruct((T, MLA_HEADS * MLA_V), BF16),
        compiler_params=_params("arbitrary", "arbitrary"),
        name="mla",
    )(q, k, v)


def _softplus(z):
    return jnp.maximum(z, 0.0) + jnp.log(1.0 + jnp.exp(-jnp.abs(z)))


def _stickbreak_body(q_ref, k_ref, v_ref, o_ref, *, S, t):
    ri = lax.broadcasted_iota(jnp.int32, (t, t), 0)
    ci = lax.broadcasted_iota(jnp.int32, (t, t), 1)
    strict = ci < ri
    upper = jnp.where(ri >= ci, 1.0, 0.0).astype(BF16)

    for i in range(S // t):
        rows = slice(i * t, (i + 1) * t)
        z = lax.dot_general(q_ref[0, rows, :], k_ref[0, 0:(i + 1) * t, :], _NT,
                            preferred_element_type=F32)
        sp = _softplus(z)
        after = jnp.zeros((t, 1), F32)
        ws = [None] * (i + 1)
        for j in range(i, -1, -1):
            cols = slice(j * t, (j + 1) * t)
            sp_j = sp[:, cols]
            if j == i:
                sp_j = jnp.where(strict, sp_j, 0.0)
            hi = sp_j.astype(BF16)
            lo = (sp_j - hi.astype(F32)).astype(BF16)
            incl = (jnp.dot(hi, upper, preferred_element_type=F32)
                    + jnp.dot(lo, upper, preferred_element_type=F32))
            w = jnp.exp(z[:, cols] - incl - after)
            if j == i:
                w = jnp.where(strict, w, 0.0)
            ws[j] = w.astype(BF16)
            after = after + incl[:, 0:1]
        w_all = ws[0] if i == 0 else jnp.concatenate(ws, axis=1)
        o = jnp.dot(w_all, v_ref[0, 0:(i + 1) * t, :], preferred_element_type=F32)
        o_ref[0, rows, :] = o.astype(BF16)


def _stickbreak(q, k, v, B, S, t):
    T = q.shape[1]
    spec = pl.BlockSpec((1, S, SB_HEAD_DIM), lambda b, h: (h, b, 0))
    return pl.pallas_call(
        functools.partial(_stickbreak_body, S=S, t=t),
        grid=(B, SB_HEADS),
        in_specs=[spec, spec, spec],
        out_specs=spec,
        out_shape=jax.ShapeDtypeStruct((SB_HEADS, T, SB_HEAD_DIM), BF16),
        compiler_params=_params("arbitrary", "arbitrary"),
        name="stickbreak",
    )(q, k, v)


def _oddeven_pairs(n):
    pairs = []
    p = 1
    while p < n:
        k = p
        while k >= 1:
            for j in range(k % p, n - k, 2 * k):
                for i in range(min(k, n - j - k)):
                    if (i + j) // (2 * p) == (i + j + k) // (2 * p):
                        pairs.append((i + j, i + j + k))
            k //= 2
        p *= 2
    return pairs


_SORT16 = _oddeven_pairs(PEER_TOPK)


def _cmpx(xs, i, j):
    a, b = xs[i], xs[j]
    if a is None:
        xs[i], xs[j] = b, None
    elif b is not None:
        xs[i], xs[j] = jnp.maximum(a, b), jnp.minimum(a, b)


def _bitonic_merge(xs):
    n = len(xs)
    d = n // 2
    while d >= 1:
        for i in range(n):
            if (i // d) % 2 == 0:
                _cmpx(xs, i, i + d)
        d //= 2


def _top16_sorted(xs):
    xs = list(xs)
    for i, j in _SORT16:
        _cmpx(xs, i, j)
    for shift in (4, 2, 1):
        ps = [None if x is None else pltpu.roll(x, shift, 0) for x in xs]
        merged = []
        for i in range(PEER_TOPK):
            a, b = xs[i], ps[PEER_TOPK - 1 - i]
            merged.append(b if a is None else a if b is None else jnp.maximum(a, b))
        xs = merged
        _bitonic_merge(xs)
    return xs


def _top16_pair_sums(v1, v2):
    sub = lax.broadcasted_iota(jnp.int32, v1[0].shape, 0)

    def pack(vals):
        out = vals[0]
        for s in range(1, SUBLANES):
            out = jnp.where(sub == s, vals[s], out)
        return out

    b_lo, b_hi = pack(v2[:SUBLANES]), pack(v2[SUBLANES:])
    cands = [v1[0] + b_lo, v1[0] + b_hi]
    for a in range(1, SUBLANES):
        s = v1[a] + b_lo
        lim = PEER_TOPK // (a + 1)
        cands.append(s if lim >= SUBLANES else jnp.where(sub < lim, s, -jnp.inf))
    cands.append(pack(v1[SUBLANES:]) + v2[0])
    cands += [None] * (PEER_TOPK - len(cands))
    return _top16_sorted(cands)


def _count_prefix(pred, vals):
    n = len(vals)
    bits = []

    def pivot(level_vals, k):
        if k == len(bits):
            return level_vals[0]
        half = len(level_vals) // 2
        return jnp.where(bits[k], pivot(level_vals[half:], k + 1), pivot(level_vals[:half], k + 1))

    step = n // 2
    while step >= 1:
        cands = [vals[base + step - 1] for base in range(0, n, 2 * step)]
        bits.append(pred(pivot(cands, 0)))
        step //= 2
    count = jnp.where(bits[0], float(n // 2), 0.0)
    for k, b in enumerate(bits[1:], start=1):
        count = count + jnp.where(b, float(n >> (k + 1)), 0.0)
    return jnp.where(pred(vals[n - 1]), float(n), count)


def _mid_body(att_ref, conv_ref, sb_ref, x_ref, g1_ref, sh2_ref, sc2_ref, lng_ref, lnb_ref,
              wo_ref, wq_ref, k1_ref, k2_ref,
              x1_ref, h2_ref, cnt_ref, e1_ref, rank_ref, e2_ref, *, alpha):
    cat = jnp.concatenate([att_ref[...], conv_ref[...]] + [sb_ref[h] for h in range(SB_HEADS)], axis=-1)
    mix = jnp.dot(cat, wo_ref[...], preferred_element_type=F32)
    x1 = _ln_noaffine(alpha * x_ref[...] + (1.0 + g1_ref[0]) * mix) * lng_ref[...] + lnb_ref[...]
    x1_ref[...] = x1
    h2 = (_ln_noaffine(x1) * (1.0 + sc2_ref[0]) + sh2_ref[0]).astype(BF16)
    h2_ref[...] = h2
    qp = jnp.dot(h2, wq_ref[...], preferred_element_type=F32).astype(BF16)
    _route(qp, k1_ref, k2_ref, cnt_ref, e1_ref, rank_ref, e2_ref)


def _route(qp, k1_ref, k2_ref, cnt_ref, e1_ref, rank_ref, e2_ref):
    for h in range(PEER_HEADS):
        o = h * 2 * PEER_HALF
        s1 = lax.dot_general(k1_ref[h], qp[:, o:o + PEER_HALF], _NT, preferred_element_type=F32)
        s2 = lax.dot_general(k2_ref[h], qp[:, o + PEER_HALF:o + 2 * PEER_HALF], _NT,
                             preferred_element_type=F32)
        split = lambda s: [s[SUBLANES * i:SUBLANES * (i + 1), :] for i in range(PEER_N_KEYS // SUBLANES)]
        v1 = _top16_sorted(split(s1))
        v2 = _top16_sorted(split(s2))
        top = _top16_pair_sums(v1, v2)
        tau = top[PEER_TOPK - 1][0:1, :]
        z = jnp.ones_like(top[0])
        for t in top[1:]:
            z = z + jnp.exp(t - top[0])
        e1_ref[h] = jnp.exp(s1 - v1[0][0:1, :]).astype(BF16)
        e2_ref[h] = (jnp.exp(s2 - v2[0][0:1, :]) * (0.5 / z[0:1, :])).astype(BF16)
        col = [v[0:1, :] for v in v2]
        cnt_ref[h] = _count_prefix(lambda vb: s1 + vb >= tau, col).astype(BF16)
        rank_ref[h] = _count_prefix(lambda vb: s2 < vb, col).astype(BF16)


def _mid(att, conv, sb, x, g1, sh2, sc2, lw, B, S, tm, alpha):
    T, D = x.shape
    nt = S // tm
    row = lambda i: (i, 0)
    per_b = lambda i: (i // nt, 0, 0)
    const2 = lambda i: (0, 0)
    const3 = lambda i: (0, 0, 0)
    route = pl.BlockSpec((PEER_HEADS, PEER_N_KEYS, tm), lambda i: (0, 0, i))
    route_shape = jax.ShapeDtypeStruct((PEER_HEADS, PEER_N_KEYS, T), BF16)
    return pl.pallas_call(
        functools.partial(_mid_body, alpha=alpha),
        grid=(T // tm,),
        in_specs=[pl.BlockSpec((tm, att.shape[1]), row), pl.BlockSpec((tm, CONV_CH), row),
                  pl.BlockSpec((SB_HEADS, tm, SB_HEAD_DIM), lambda i: (0, i, 0)),
                  pl.BlockSpec((tm, D), row),
                  pl.BlockSpec((1, 1, D), per_b), pl.BlockSpec((1, 1, D), per_b),
                  pl.BlockSpec((1, 1, D), per_b),
                  pl.BlockSpec((1, D), const2), pl.BlockSpec((1, D), const2),
                  pl.BlockSpec(lw["w_o"].shape, const2), pl.BlockSpec(lw["wq"].shape, const2),
                  pl.BlockSpec(lw["k1"].shape, const3), pl.BlockSpec(lw["k2"].shape, const3)],
        out_specs=[pl.BlockSpec((tm, D), row), pl.BlockSpec((tm, D), row), route, route, route, route],
        out_shape=[jax.ShapeDtypeStruct((T, D), F32), jax.ShapeDtypeStruct((T, D), BF16),
                   route_shape, route_shape, route_shape, route_shape],
        compiler_params=_params("arbitrary"),
        name="mid",
    )(att, conv, sb, x, g1, sh2, sc2, lw["ln1_g"], lw["ln1_b"], lw["w_o"], lw["wq"],
      lw["k1"], lw["k2"])


def _peer_body(h2_ref, u_ref, vt_ref, cnt_ref, e1_ref, rank_ref, e2_ref, x1_ref, g2_ref, lng_ref,
               lnb_ref, o_ref, a_ref, w_ref, acc_ref, *, tm, rows, alpha):
    j = pl.program_id(1)

    @pl.when(j == 0)
    def _():
        acc_ref[...] = jnp.zeros_like(acc_ref)

    for e0 in range(0, rows * PEER_N_KEYS, 1024):
        for t0 in range(0, tm, 2 * LANES):
            a_ref[e0:e0 + 1024, t0:t0 + 2 * LANES] = lax.dot_general(
                u_ref[e0:e0 + 1024, :], h2_ref[t0:t0 + 2 * LANES, :], _NT,
                preferred_element_type=F32)

    for rr in range(rows):
        rs = slice(rr * PEER_N_KEYS, (rr + 1) * PEER_N_KEYS)
        tw = 2 * LANES
        for ts in range(tm // tw):
            ln = slice(ts * tw, (ts + 1) * tw)
            g = jnp.zeros((PEER_N_KEYS, tw), BF16)
            for h in range(PEER_HEADS):
                cnt = cnt_ref[h, rr:rr + 1, ln]
                e1 = e1_ref[h, rr:rr + 1, ln]
                e2 = e2_ref[h, :, ln]
                g = g + jnp.where(rank_ref[h, :, ln] < cnt, e2, jnp.zeros_like(e2)) * e1
            a = a_ref[rs, ln].astype(BF16)
            act = a * (1.0 + lax.erf(a * SQRT_HALF))
            w_ref[rs, ln] = act * g
    acc_ref[...] += lax.dot_general(vt_ref[...], w_ref[...], (((0,), (0,)), ((), ())),
                                    preferred_element_type=F32)

    @pl.when(j == pl.num_programs(1) - 1)
    def _():
        ffn = acc_ref[...].T
        y = alpha * x1_ref[...] + (1.0 + g2_ref[0]) * ffn
        o_ref[...] = _ln_noaffine(y) * lng_ref[...] + lnb_ref[...]


def _peer(h2, cnt, e1, rank, e2, x1, g2, lw, B, S, tm, rows, alpha):
    T, D = x1.shape
    nt = S // tm
    ne = rows * PEER_N_KEYS
    n_steps = PEER_N_KEYS // rows
    row = lambda i, j: (i, 0)
    const2 = lambda i, j: (0, 0)
    return pl.pallas_call(
        functools.partial(_peer_body, tm=tm, rows=rows, alpha=alpha),
        grid=(T // tm, n_steps),
        in_specs=[pl.BlockSpec((tm, D), row),
                  pl.BlockSpec((ne, D), lambda i, j: (j, 0)),
                  pl.BlockSpec((ne, D), lambda i, j: (j, 0)),
                  pl.BlockSpec((PEER_HEADS, rows, tm), lambda i, j: (0, j, i)),
                  pl.BlockSpec((PEER_HEADS, rows, tm), lambda i, j: (0, j, i)),
                  pl.BlockSpec((PEER_HEADS, PEER_N_KEYS, tm), lambda i, j: (0, 0, i)),
                  pl.BlockSpec((PEER_HEADS, PEER_N_KEYS, tm), lambda i, j: (0, 0, i)),
                  pl.BlockSpec((tm, D), row),
                  pl.BlockSpec((1, 1, D), lambda i, j: (i // nt, 0, 0)),
                  pl.BlockSpec((1, D), const2), pl.BlockSpec((1, D), const2)],
        out_specs=pl.BlockSpec((tm, D), row),
        out_shape=jax.ShapeDtypeStruct((T, D), F32),
        scratch_shapes=[pltpu.VMEM((ne, tm), F32), pltpu.VMEM((ne, tm), BF16),
                        pltpu.VMEM((D, tm), F32)],
        compiler_params=_params("arbitrary", "arbitrary"),
        name="peer",
    )(h2, lw["u"], lw["vt"], cnt, e1, rank, e2, x1, g2, lw["ln2_g"], lw["ln2_b"])


def _layer_weights(l, w_in, q_norm_g, kv_norm_g, w_uq, w_ukv, conv_w, w_o, ln1_g, ln1_b,
                   peer_wq, peer_k1, peer_k2, peer_u, peer_v, ln2_g, ln2_b):
    D = D_MODEL
    wi = w_in[l]
    pad = jnp.zeros((D, LANES - MLA_ROPE), F32)
    w_in_p = jnp.concatenate([wi[:, :416], pad, wi[:, 416:]], axis=1).astype(BF16)
    qk = MLA_NOPE + MLA_ROPE
    w_uq_p = jnp.pad(w_uq[l].reshape(MLA_Q_RANK, MLA_HEADS, qk),
                     ((0, 0), (0, 0), (0, HEAD_PAD - qk))).reshape(MLA_Q_RANK, -1).astype(BF16)
    ukv = w_ukv[l].reshape(MLA_KV_RANK, MLA_HEADS, MLA_NOPE + MLA_V)
    wk = jnp.pad(ukv[:, :, :MLA_NOPE], ((0, 0), (0, 0), (0, HEAD_PAD - MLA_NOPE)))
    expand = np.zeros((LANES, MLA_HEADS, HEAD_PAD), np.float32)
    for r in range(MLA_ROPE):
        expand[r, :, MLA_NOPE + r] = 1.0
    wk2 = jnp.concatenate([wk, jnp.asarray(expand)], axis=0).reshape(MLA_KV_RANK + LANES, -1).astype(BF16)
    wv = ukv[:, :, MLA_NOPE:].reshape(MLA_KV_RANK, -1).astype(BF16)
    return dict(
        w_in=w_in_p, qg=q_norm_g[l].reshape(1, -1), kvg=kv_norm_g[l].reshape(1, -1),
        w_uq=w_uq_p, wk2=wk2, wv=wv, conv_w=conv_w[l],
        w_o=w_o[l].astype(BF16), ln1_g=ln1_g[l].reshape(1, D), ln1_b=ln1_b[l].reshape(1, D),
        wq=peer_wq[l].astype(BF16), k1=peer_k1[l].astype(BF16), k2=peer_k2[l].astype(BF16),
        u=peer_u[l].astype(BF16), vt=peer_v[l].astype(BF16),
        ln2_g=ln2_g[l].reshape(1, D), ln2_b=ln2_b[l].reshape(1, D))


def _forward(x, c, positions, ada_w, ada_b, *weights, tm_in, tq_mla, t_sb, tm_mid, tm_peer, rows):
    B, S, D = x.shape
    T = B * S
    depth = ada_w.shape[0]
    alpha = (2.0 * depth) ** 0.25
    mod = _adaln(c, ada_w, ada_b)
    half = MLA_ROPE // 2
    inv_freq = ROPE_THETA ** (-jnp.arange(half, dtype=F32) / half)
    invf = jnp.zeros((LANES,), F32).at[MLA_NOPE:MLA_NOPE + half].set(inv_freq)
    invf = invf.at[MLA_NOPE + half:MLA_NOPE + MLA_ROPE].set(inv_freq).reshape(1, LANES)
    cos, sin = _rope_tables(positions.reshape(T, 1), invf, S)
    xf = x.reshape(T, D)
    for l in range(depth):
        lw = _layer_weights(l, *weights)
        sh1, sc1, g1, sh2, sc2, g2 = [m.reshape(B, 1, D) for m in jnp.split(mod[l], 6, axis=-1)]
        q, k, v, conv, sbq, sbk, sbv = _mixer_in(xf, cos, sin, sh1, sc1, lw, B, S, tm_in)
        att = _mla(q, k, v, B, S, tq_mla)
        sbo = _stickbreak(sbq, sbk, sbv, B, S, t_sb)
        x1, h2, cnt, e1, rank, e2 = _mid(att, conv, sbo, xf, g1, sh2, sc2, lw, B, S, tm_mid, alpha)
        xf = _peer(h2, cnt, e1, rank, e2, x1, g2, lw, B, S, tm_peer, rows, alpha)
    return xf.reshape(B, S, D)


def kernel(x, c, positions, ada_w, ada_b, w_in, q_norm_g, kv_norm_g, w_uq, w_ukv, conv_w, w_o,
           ln1_g, ln1_b, peer_wq, peer_k1, peer_k2, peer_u, peer_v, ln2_g, ln2_b):
    return _forward(x, c, positions, ada_w, ada_b, w_in, q_norm_g, kv_norm_g, w_uq, w_ukv, conv_w,
                    w_o, ln1_g, ln1_b, peer_wq, peer_k1, peer_k2, peer_u, peer_v, ln2_g, ln2_b,
                    tm_in=512, tq_mla=256, t_sb=256, tm_mid=256, tm_peer=512, rows=16)
```

```python
import functools
import math

import numpy as np
import jax
import jax.numpy as jnp
from jax import lax
from jax.experimental import pallas as pl
from jax.experimental.pallas import tpu as pltpu

F32 = jnp.float32
BF16 = jnp.bfloat16

D_MODEL = 1024
CHUNK = 64

MLA_HEADS = 8
MLA_NOPE = 64
MLA_ROPE = 32
MLA_V = 64
MLA_Q_RANK = 256
MLA_KV_RANK = 128
ROPE_THETA = 10000.0
HEAD_PAD = 128

CONV_CH = 256
CONV_WIDTH = 3
SB_HEADS = 4
SB_HEAD_DIM = 64
SB_WIDTH = 256

PEER_HEADS = 8
PEER_N_KEYS = 128
PEER_HALF = 128
PEER_TOPK = 16

LN_EPS = 1e-5
RMS_EPS = 1e-6
NEG_INF = -1e30
SQRT_HALF = 0.7071067811865476

LANES = 128
SUBLANES = 8
VMEM_LIMIT = 56 * 1024 * 1024

_O_QLAT, _O_KVLAT, _O_KR = 0, 256, 384
_O_CB, _O_CC, _O_CH = 512, 768, 1024
_O_SQ, _O_SK, _O_SV = 1280, 1536, 1792
IN_PAD = 2048

_NT = (((1,), (1,)), ((), ()))


def _ln_noaffine(x):
    mu = jnp.mean(x, axis=-1, keepdims=True)
    xc = x - mu
    var = jnp.mean(xc * xc, axis=-1, keepdims=True)
    return xc * lax.rsqrt(var + LN_EPS)


def _rms(x, gain):
    return x * lax.rsqrt(jnp.mean(x * x, axis=-1, keepdims=True) + RMS_EPS) * gain


def _params(*sem):
    return pltpu.CompilerParams(dimension_semantics=sem, vmem_limit_bytes=VMEM_LIMIT)


def _adaln_body(c_ref, w_ref, b_ref, o_ref):
    c = c_ref[...]
    ca = c * jax.nn.sigmoid(c)
    o_ref[0] = jnp.dot(ca, w_ref[0], preferred_element_type=F32,
                       precision=lax.Precision.HIGHEST) + b_ref[0]


def _adaln(c, ada_w, ada_b):
    L, D, N = ada_w.shape
    B = c.shape[0]
    tn = 1024
    return pl.pallas_call(
        _adaln_body,
        grid=(L, N // tn),
        in_specs=[pl.BlockSpec((B, D), lambda l, n: (0, 0)),
                  pl.BlockSpec((1, D, tn), lambda l, n: (l, 0, n)),
                  pl.BlockSpec((1, 1, tn), lambda l, n: (l, 0, n))],
        out_specs=pl.BlockSpec((1, B, tn), lambda l, n: (l, 0, n)),
        out_shape=jax.ShapeDtypeStruct((L, B, N), F32),
        compiler_params=_params("arbitrary", "arbitrary"),
        name="adaln",
    )(c, ada_w, ada_b.reshape(L, 1, N))


def _rope_tables_body(pos_ref, invf_ref, cos_ref, sin_ref):
    ang = pos_ref[...].astype(F32) * invf_ref[...]
    cos_ref[...] = jnp.cos(ang)
    sin_ref[...] = jnp.sin(ang)


def _rope_tables(pos, invf, tm):
    T = pos.shape[0]
    row = lambda i: (i, 0)
    tab = jax.ShapeDtypeStruct((T, LANES), F32)
    return pl.pallas_call(
        _rope_tables_body,
        grid=(T // tm,),
        in_specs=[pl.BlockSpec((tm, 1), row), pl.BlockSpec((1, LANES), lambda i: (0, 0))],
        out_specs=[pl.BlockSpec((tm, LANES), row)] * 2,
        out_shape=[tab, tab],
        compiler_params=_params("arbitrary"),
        name="rope_tables",
    )(pos, invf)


def _mixer_in_body(cos_ref, sin_ref, x_ref, sh_ref, sc_ref, win_ref, qg_ref, kvg_ref, wuq_ref,
                   wk2_ref, wv_ref, cw_ref,
                   q_ref, k_ref, v_ref, conv_ref, sbq_ref, sbk_ref, sbv_ref, ubuf_ref, *, tm):
    j = pl.program_id(1)
    h = _ln_noaffine(x_ref[...]) * (1.0 + sc_ref[0]) + sh_ref[0]
    proj = jnp.dot(h.astype(BF16), win_ref[...], preferred_element_type=F32)

    qn = _rms(proj[:, _O_QLAT:_O_QLAT + MLA_Q_RANK], qg_ref[...])
    kvn = _rms(proj[:, _O_KVLAT:_O_KVLAT + MLA_KV_RANK], kvg_ref[...])
    kr = proj[:, _O_KR:_O_KR + LANES]
    q = jnp.dot(qn.astype(BF16), wuq_ref[...], preferred_element_type=F32)
    kcat = jnp.concatenate([kvn, kr], axis=-1).astype(BF16)
    k = jnp.dot(kcat, wk2_ref[...], preferred_element_type=F32)
    v = jnp.dot(kvn.astype(BF16), wv_ref[...], preferred_element_type=F32)
    v_ref[...] = v.astype(BF16)

    cos = cos_ref[...]
    sin = sin_ref[...]
    lane = lax.broadcasted_iota(jnp.int32, (1, LANES), 1)
    half = MLA_ROPE // 2
    first = (lane >= MLA_NOPE) & (lane < MLA_NOPE + half)
    second = (lane >= MLA_NOPE + half) & (lane < MLA_NOPE + MLA_ROPE)
    s_prev = jnp.where(second, sin, 0.0)
    s_next = jnp.where(first, -sin, 0.0)
    qscale = (MLA_NOPE + MLA_ROPE) ** -0.5
    for hb in range(MLA_HEADS):
        sl = slice(hb * HEAD_PAD, (hb + 1) * HEAD_PAD)
        for src, dst, scale in ((q, q_ref, qscale), (k, k_ref, None)):
            xb = src[:, sl]
            y = (xb * cos + pltpu.roll(xb, half, 1) * s_prev
                 + pltpu.roll(xb, LANES - half, 1) * s_next)
            if scale is not None:
                y = y * scale
            dst[:, sl] = y.astype(BF16)

    cb = proj[:, _O_CB:_O_CB + CONV_CH]
    u = proj[:, _O_CC:_O_CC + CONV_CH] * proj[:, _O_CH:_O_CH + CONV_CH]

    @pl.when(j == 0)
    def _():
        ubuf_ref[0:SUBLANES, :] = jnp.zeros((SUBLANES, CONV_CH), F32)

    ubuf_ref[SUBLANES:SUBLANES + tm, :] = u
    y = (cw_ref[0:1, :] * ubuf_ref[SUBLANES - 2:SUBLANES - 2 + tm, :]
         + cw_ref[1:2, :] * ubuf_ref[SUBLANES - 1:SUBLANES - 1 + tm, :]
         + cw_ref[2:3, :] * u)
    conv_ref[...] = (cb * y).astype(BF16)
    ubuf_ref[0:SUBLANES, :] = ubuf_ref[tm:tm + SUBLANES, :]

    for hh in range(SB_HEADS):
        o = hh * SB_HEAD_DIM
        sbq_ref[hh] = (proj[:, _O_SQ + o:_O_SQ + o + SB_HEAD_DIM] * (SB_HEAD_DIM ** -0.5)).astype(BF16)
        sbk_ref[hh] = proj[:, _O_SK + o:_O_SK + o + SB_HEAD_DIM].astype(BF16)
        sbv_ref[hh] = proj[:, _O_SV + o:_O_SV + o + SB_HEAD_DIM].astype(BF16)


def _mixer_in(x, cos, sin, sh, sc, lw, B, S, tm):
    T, D = x.shape
    nt = S // tm
    row = lambda b, j: (b * nt + j, 0)
    per_b = lambda b, j: (b, 0, 0)
    const = lambda b, j: (0, 0)
    full = lambda a: pl.BlockSpec(a.shape, const)
    outs = [(MLA_HEADS * HEAD_PAD, BF16), (MLA_HEADS * HEAD_PAD, BF16), (MLA_HEADS * MLA_V, BF16),
            (CONV_CH, BF16)]
    sb_spec = pl.BlockSpec((SB_HEADS, tm, SB_HEAD_DIM), lambda b, j: (0, b * nt + j, 0))
    sb_shape = jax.ShapeDtypeStruct((SB_HEADS, T, SB_HEAD_DIM), BF16)
    return pl.pallas_call(
        functools.partial(_mixer_in_body, tm=tm),
        grid=(B, nt),
        in_specs=[pl.BlockSpec((tm, LANES), row), pl.BlockSpec((tm, LANES), row),
                  pl.BlockSpec((tm, D), row),
                  pl.BlockSpec((1, 1, D), per_b), pl.BlockSpec((1, 1, D), per_b),
                  full(lw["w_in"]), full(lw["qg"]), full(lw["kvg"]), full(lw["w_uq"]),
                  full(lw["wk2"]), full(lw["wv"]), full(lw["conv_w"])],
        out_specs=[pl.BlockSpec((tm, n), row) for n, _ in outs] + [sb_spec] * 3,
        out_shape=[jax.ShapeDtypeStruct((T, n), dt) for n, dt in outs] + [sb_shape] * 3,
        scratch_shapes=[pltpu.VMEM((tm + SUBLANES, CONV_CH), F32)],
        compiler_params=_params("arbitrary", "arbitrary"),
        name="mixer_in",
    )(cos, sin, x, sh, sc, lw["w_in"], lw["qg"], lw["kvg"], lw["w_uq"], lw["wk2"], lw["wv"],
      lw["conv_w"])


def _mla_body(q_ref, k_ref, v_ref, o_ref, *, S, tq):
    r = lax.broadcasted_iota(jnp.int32, (tq, tq), 0) // CHUNK
    c = lax.broadcasted_iota(jnp.int32, (tq, tq), 1) // CHUNK
    visible = c <= r

    for hh in range(2):
        hs = slice(hh * HEAD_PAD, (hh + 1) * HEAD_PAD)
        vs = slice(hh * MLA_V, (hh + 1) * MLA_V)
        for i in range(S // tq):
            rows = slice(i * tq, (i + 1) * tq)
            q = q_ref[rows, hs]
            s_d = lax.dot_general(q, k_ref[rows, hs], _NT, preferred_element_type=F32)
            s_d = jnp.where(visible, s_d, NEG_INF)
            m = jnp.max(s_d, axis=-1, keepdims=True)
            if i > 0:
                s_o = lax.dot_general(q, k_ref[0:i * tq, hs], _NT, preferred_element_type=F32)
                m = jnp.maximum(m, jnp.max(s_o, axis=-1, keepdims=True))
            p_d = jnp.exp(s_d - m)
            l = jnp.sum(p_d, axis=-1, keepdims=True)
            acc = jnp.dot(p_d.astype(BF16), v_ref[rows, vs], preferred_element_type=F32)
            if i > 0:
                p_o = jnp.exp(s_o - m)
                l = l + jnp.sum(p_o, axis=-1, keepdims=True)
                acc = acc + jnp.dot(p_o.astype(BF16), v_ref[0:i * tq, vs], preferred_element_type=F32)
            o_ref[rows, vs] = (acc / l).astype(BF16)


def _mla(q, k, v, B, S, tq):
    T = q.shape[0]
    hp = MLA_HEADS // 2
    return pl.pallas_call(
        functools.partial(_mla_body, S=S, tq=tq),
        grid=(B, hp),
        in_specs=[pl.BlockSpec((S, 2 * HEAD_PAD), lambda b, h: (b, h)),
                  pl.BlockSpec((S, 2 * HEAD_PAD), lambda b, h: (b, h)),
                  pl.BlockSpec((S, 2 * MLA_V), lambda b, h: (b, h))],
        out_specs=pl.BlockSpec((S, 2 * MLA_V), lambda b, h: (b, h)),
        out_shape=jax.ShapeDtypeStruct((T, MLA_HEADS * MLA_V), BF16),
        compiler_params=_params("arbitrary", "arbitrary"),
        name="mla",
    )(q, k, v)


def _softplus(z):
    return jnp.maximum(z, 0.0) + jnp.log(1.0 + jnp.exp(-jnp.abs(z)))


def _stickbreak_body(q_ref, k_ref, v_ref, o_ref, *, S, t):
    ri = lax.broadcasted_iota(jnp.int32, (t, t), 0)
    ci = lax.broadcasted_iota(jnp.int32, (t, t), 1)
    strict = ci < ri
    upper = jnp.where(ri >= ci, 1.0, 0.0).astype(BF16)

    for i in range(S // t):
        rows = slice(i * t, (i + 1) * t)
        z = lax.dot_general(q_ref[0, rows, :], k_ref[0, 0:(i + 1) * t, :], _NT,
                            preferred_element_type=F32)
        sp = _softplus(z)
        after = jnp.zeros((t, 1), F32)
        ws = [None] * (i + 1)
        for j in range(i, -1, -1):
            cols = slice(j * t, (j + 1) * t)
            sp_j = sp[:, cols]
            if j == i:
                sp_j = jnp.where(strict, sp_j, 0.0)
            hi = sp_j.astype(BF16)
            lo = (sp_j - hi.astype(F32)).astype(BF16)
            incl = (jnp.dot(hi, upper, preferred_element_type=F32)
                    + jnp.dot(lo, upper, preferred_element_type=F32))
            w = jnp.exp(z[:, cols] - incl - after)
            if j == i:
                w = jnp.where(strict, w, 0.0)
            ws[j] = w.astype(BF16)
            after = after + incl[:, 0:1]
        w_all = ws[0] if i == 0 else jnp.concatenate(ws, axis=1)
        o = jnp.dot(w_all, v_ref[0, 0:(i + 1) * t, :], preferred_element_type=F32)
        o_ref[0, rows, :] = o.astype(BF16)


def _stickbreak(q, k, v, B, S, t):
    T = q.shape[1]
    spec = pl.BlockSpec((1, S, SB_HEAD_DIM), lambda b, h: (h, b, 0))
    return pl.pallas_call(
        functools.partial(_stickbreak_body, S=S, t=t),
        grid=(B, SB_HEADS),
        in_specs=[spec, spec, spec],
        out_specs=spec,
        out_shape=jax.ShapeDtypeStruct((SB_HEADS, T, SB_HEAD_DIM), BF16),
        compiler_params=_params("arbitrary", "arbitrary"),
        name="stickbreak",
    )(q, k, v)


def _oddeven_pairs(n):
    pairs = []
    p = 1
    while p < n:
        k = p
        while k >= 1:
            for j in range(k % p, n - k, 2 * k):
                for i in range(min(k, n - j - k)):
                    if (i + j) // (2 * p) == (i + j + k) // (2 * p):
                        pairs.append((i + j, i + j + k))
            k //= 2
        p *= 2
    return pairs


_SORT16 = _oddeven_pairs(PEER_TOPK)


def _cmpx(xs, i, j):
    a, b = xs[i], xs[j]
    if a is None:
        xs[i], xs[j] = b, None
    elif b is not None:
        xs[i], xs[j] = jnp.maximum(a, b), jnp.minimum(a, b)


def _bitonic_merge(xs):
    n = len(xs)
    d = n // 2
    while d >= 1:
        for i in range(n):
            if (i // d) % 2 == 0:
                _cmpx(xs, i, i + d)
        d //= 2


def _top16_sorted(xs):
    xs = list(xs)
    for i, j in _SORT16:
        _cmpx(xs, i, j)
    for shift in (4, 2, 1):
        ps = [None if x is None else pltpu.roll(x, shift, 0) for x in xs]
        merged = []
        for i in range(PEER_TOPK):
            a, b = xs[i], ps[PEER_TOPK - 1 - i]
            merged.append(b if a is None else a if b is None else jnp.maximum(a, b))
        xs = merged
        _bitonic_merge(xs)
    return xs


def _top16_pair_sums(v1, v2):
    sub = lax.broadcasted_iota(jnp.int32, v1[0].shape, 0)

    def pack(vals):
        out = vals[0]
        for s in range(1, SUBLANES):
            out = jnp.where(sub == s, vals[s], out)
        return out

    b_lo, b_hi = pack(v2[:SUBLANES]), pack(v2[SUBLANES:])
    cands = [v1[0] + b_lo, v1[0] + b_hi]
    for a in range(1, SUBLANES):
        s = v1[a] + b_lo
        lim = PEER_TOPK // (a + 1)
        cands.append(s if lim >= SUBLANES else jnp.where(sub < lim, s, -jnp.inf))
    cands.append(pack(v1[SUBLANES:]) + v2[0])
    cands += [None] * (PEER_TOPK - len(cands))
    return _top16_sorted(cands)


def _count_prefix(pred, vals):
    n = len(vals)
    bits = []

    def pivot(level_vals, k):
        if k == len(bits):
            return level_vals[0]
        half = len(level_vals) // 2
        return jnp.where(bits[k], pivot(level_vals[half:], k + 1), pivot(level_vals[:half], k + 1))

    step = n // 2
    while step >= 1:
        cands = [vals[base + step - 1] for base in range(0, n, 2 * step)]
        bits.append(pred(pivot(cands, 0)))
        step //= 2
    count = jnp.where(bits[0], float(n // 2), 0.0)
    for k, b in enumerate(bits[1:], start=1):
        count = count + jnp.where(b, float(n >> (k + 1)), 0.0)
    return jnp.where(pred(vals[n - 1]), float(n), count)


def _mid_body(att_ref, conv_ref, sb_ref, x_ref, g1_ref, sh2_ref, sc2_ref, lng_ref, lnb_ref,
              wo_ref, wq_ref, k1_ref, k2_ref,
              x1_ref, h2_ref, cnt_ref, e1_ref, rank_ref, e2_ref, *, alpha):
    cat = jnp.concatenate([att_ref[...], conv_ref[...]] + [sb_ref[h] for h in range(SB_HEADS)], axis=-1)
    mix = jnp.dot(cat, wo_ref[...], preferred_element_type=F32)
    x1 = _ln_noaffine(alpha * x_ref[...] + (1.0 + g1_ref[0]) * mix) * lng_ref[...] + lnb_ref[...]
    x1_ref[...] = x1
    h2 = (_ln_noaffine(x1) * (1.0 + sc2_ref[0]) + sh2_ref[0]).astype(BF16)
    h2_ref[...] = h2
    qp = jnp.dot(h2, wq_ref[...], preferred_element_type=F32).astype(BF16)
    _route(qp, k1_ref, k2_ref, cnt_ref, e1_ref, rank_ref, e2_ref)


def _route(qp, k1_ref, k2_ref, cnt_ref, e1_ref, rank_ref, e2_ref):
    for h in range(PEER_HEADS):
        o = h * 2 * PEER_HALF
        s1 = lax.dot_general(k1_ref[h], qp[:, o:o + PEER_HALF], _NT, preferred_element_type=F32)
        s2 = lax.dot_general(k2_ref[h], qp[:, o + PEER_HALF:o + 2 * PEER_HALF], _NT,
                             preferred_element_type=F32)
        split = lambda s: [s[SUBLANES * i:SUBLANES * (i + 1), :] for i in range(PEER_N_KEYS // SUBLANES)]
        v1 = _top16_sorted(split(s1))
        v2 = _top16_sorted(split(s2))
        top = _top16_pair_sums(v1, v2)
        tau = top[PEER_TOPK - 1][0:1, :]
        z = jnp.ones_like(top[0])
        for t in top[1:]:
            z = z + jnp.exp(t - top[0])
        e1_ref[h] = jnp.exp(s1 - v1[0][0:1, :]).astype(BF16)
        e2_ref[h] = (jnp.exp(s2 - v2[0][0:1, :]) * (0.5 / z[0:1, :])).astype(BF16)
        col = [v[0:1, :] for v in v2]
        cnt_ref[h] = _count_prefix(lambda vb: s1 + vb >= tau, col).astype(BF16)
        rank_ref[h] = _count_prefix(lambda vb: s2 < vb, col).astype(BF16)


def _mid(att, conv, sb, x, g1, sh2, sc2, lw, B, S, tm, alpha):
    T, D = x.shape
    nt = S // tm
    row = lambda i: (i, 0)
    per_b = lambda i: (i // nt, 0, 0)
    const2 = lambda i: (0, 0)
    const3 = lambda i: (0, 0, 0)
    route = pl.BlockSpec((PEER_HEADS, PEER_N_KEYS, tm), lambda i: (0, 0, i))
    route_shape = jax.ShapeDtypeStruct((PEER_HEADS, PEER_N_KEYS, T), BF16)
    return pl.pallas_call(
        functools.partial(_mid_body, alpha=alpha),
        grid=(T // tm,),
        in_specs=[pl.BlockSpec((tm, att.shape[1]), row), pl.BlockSpec((tm, CONV_CH), row),
                  pl.BlockSpec((SB_HEADS, tm, SB_HEAD_DIM), lambda i: (0, i, 0)),
                  pl.BlockSpec((tm, D), row),
                  pl.BlockSpec((1, 1, D), per_b), pl.BlockSpec((1, 1, D), per_b),
                  pl.BlockSpec((1, 1, D), per_b),
                  pl.BlockSpec((1, D), const2), pl.BlockSpec((1, D), const2),
                  pl.BlockSpec(lw["w_o"].shape, const2), pl.BlockSpec(lw["wq"].shape, const2),
                  pl.BlockSpec(lw["k1"].shape, const3), pl.BlockSpec(lw["k2"].shape, const3)],
        out_specs=[pl.BlockSpec((tm, D), row), pl.BlockSpec((tm, D), row), route, route, route, route],
        out_shape=[jax.ShapeDtypeStruct((T, D), F32), jax.ShapeDtypeStruct((T, D), BF16),
                   route_shape, route_shape, route_shape, route_shape],
        compiler_params=_params("arbitrary"),
        name="mid",
    )(att, conv, sb, x, g1, sh2, sc2, lw["ln1_g"], lw["ln1_b"], lw["w_o"], lw["wq"],
      lw["k1"], lw["k2"])


def _peer_body(h2_ref, u_ref, vt_ref, cnt_ref, e1_ref, rank_ref, e2_ref, x1_ref, g2_ref, lng_ref,
               lnb_ref, o_ref, a_ref, w_ref, acc_ref, *, tm, rows, alpha):
    j = pl.program_id(1)

    @pl.when(j == 0)
    def _():
        acc_ref[...] = jnp.zeros_like(acc_ref)

    for e0 in range(0, rows * PEER_N_KEYS, 1024):
        for t0 in range(0, tm, 2 * LANES):
            a_ref[e0:e0 + 1024, t0:t0 + 2 * LANES] = lax.dot_general(
                u_ref[e0:e0 + 1024, :], h2_ref[t0:t0 + 2 * LANES, :], _NT,
                preferred_element_type=F32)

    for rr in range(rows):
        rs = slice(rr * PEER_N_KEYS, (rr + 1) * PEER_N_KEYS)
        tw = 2 * LANES
        for ts in range(tm // tw):
            ln = slice(ts * tw, (ts + 1) * tw)
            g = jnp.zeros((PEER_N_KEYS, tw), BF16)
            for h in range(PEER_HEADS):
                cnt = cnt_ref[h, rr:rr + 1, ln]
                e1 = e1_ref[h, rr:rr + 1, ln]
                e2 = e2_ref[h, :, ln]
                g = g + jnp.where(rank_ref[h, :, ln] < cnt, e2, jnp.zeros_like(e2)) * e1
            a = a_ref[rs, ln].astype(BF16)
            act = a * (1.0 + lax.erf(a * SQRT_HALF))
            w_ref[rs, ln] = act * g
    acc_ref[...] += jnp.dot(vt_ref[...], w_ref[...], preferred_element_type=F32)

    @pl.when(j == pl.num_programs(1) - 1)
    def _():
        ffn = acc_ref[...].T
        y = alpha * x1_ref[...] + (1.0 + g2_ref[0]) * ffn
        o_ref[...] = _ln_noaffine(y) * lng_ref[...] + lnb_ref[...]


def _peer(h2, cnt, e1, rank, e2, x1, g2, lw, B, S, tm, rows, alpha):
    T, D = x1.shape
    nt = S // tm
    ne = rows * PEER_N_KEYS
    n_steps = PEER_N_KEYS // rows
    row = lambda i, j: (i, 0)
    const2 = lambda i, j: (0, 0)
    return pl.pallas_call(
        functools.partial(_peer_body, tm=tm, rows=rows, alpha=alpha),
        grid=(T // tm, n_steps),
        in_specs=[pl.BlockSpec((tm, D), row),
                  pl.BlockSpec((ne, D), lambda i, j: (j, 0)),
                  pl.BlockSpec((D, ne), lambda i, j: (0, j)),
                  pl.BlockSpec((PEER_HEADS, rows, tm), lambda i, j: (0, j, i)),
                  pl.BlockSpec((PEER_HEADS, rows, tm), lambda i, j: (0, j, i)),
                  pl.BlockSpec((PEER_HEADS, PEER_N_KEYS, tm), lambda i, j: (0, 0, i)),
                  pl.BlockSpec((PEER_HEADS, PEER_N_KEYS, tm), lambda i, j: (0, 0, i)),
                  pl.BlockSpec((tm, D), row),
                  pl.BlockSpec((1, 1, D), lambda i, j: (i // nt, 0, 0)),
                  pl.BlockSpec((1, D), const2), pl.BlockSpec((1, D), const2)],
        out_specs=pl.BlockSpec((tm, D), row),
        out_shape=jax.ShapeDtypeStruct((T, D), F32),
        scratch_shapes=[pltpu.VMEM((ne, tm), F32), pltpu.VMEM((ne, tm), BF16),
                        pltpu.VMEM((D, tm), F32)],
        compiler_params=_params("arbitrary", "arbitrary"),
        name="peer",
    )(h2, lw["u"], lw["vt"], cnt, e1, rank, e2, x1, g2, lw["ln2_g"], lw["ln2_b"])


def _layer_weights(l, w_in, q_norm_g, kv_norm_g, w_uq, w_ukv, conv_w, w_o, ln1_g, ln1_b,
                   peer_wq, peer_k1, peer_k2, peer_u, peer_v, ln2_g, ln2_b):
    D = D_MODEL
    wi = w_in[l]
    pad = jnp.zeros((D, LANES - MLA_ROPE), F32)
    w_in_p = jnp.concatenate([wi[:, :416], pad, wi[:, 416:]], axis=1).astype(BF16)
    qk = MLA_NOPE + MLA_ROPE
    w_uq_p = jnp.pad(w_uq[l].reshape(MLA_Q_RANK, MLA_HEADS, qk),
                     ((0, 0), (0, 0), (0, HEAD_PAD - qk))).reshape(MLA_Q_RANK, -1).astype(BF16)
    ukv = w_ukv[l].reshape(MLA_KV_RANK, MLA_HEADS, MLA_NOPE + MLA_V)
    wk = jnp.pad(ukv[:, :, :MLA_NOPE], ((0, 0), (0, 0), (0, HEAD_PAD - MLA_NOPE)))
    expand = np.zeros((LANES, MLA_HEADS, HEAD_PAD), np.float32)
    for r in range(MLA_ROPE):
        expand[r, :, MLA_NOPE + r] = 1.0
    wk2 = jnp.concatenate([wk, jnp.asarray(expand)], axis=0).reshape(MLA_KV_RANK + LANES, -1).astype(BF16)
    wv = ukv[:, :, MLA_NOPE:].reshape(MLA_KV_RANK, -1).astype(BF16)
    return dict(
        w_in=w_in_p, qg=q_norm_g[l].reshape(1, -1), kvg=kv_norm_g[l].reshape(1, -1),
        w_uq=w_uq_p, wk2=wk2, wv=wv, conv_w=conv_w[l],
        w_o=w_o[l].astype(BF16), ln1_g=ln1_g[l].reshape(1, D), ln1_b=ln1_b[l].reshape(1, D),
        wq=peer_wq[l].astype(BF16), k1=peer_k1[l].astype(BF16), k2=peer_k2[l].astype(BF16),
        u=peer_u[l].astype(BF16), vt=peer_v[l].T.astype(BF16),
        ln2_g=ln2_g[l].reshape(1, D), ln2_b=ln2_b[l].reshape(1, D))


def _forward(x, c, positions, ada_w, ada_b, *weights, tm_in, tq_mla, t_sb, tm_mid, tm_peer, rows):
    B, S, D = x.shape
    T = B * S
    depth = ada_w.shape[0]
    alpha = (2.0 * depth) ** 0.25
    mod = _adaln(c, ada_w, ada_b)
    half = MLA_ROPE // 2
    inv_freq = ROPE_THETA ** (-jnp.arange(half, dtype=F32) / half)
    invf = jnp.zeros((LANES,), F32).at[MLA_NOPE:MLA_NOPE + half].set(inv_freq)
    invf = invf.at[MLA_NOPE + half:MLA_NOPE + MLA_ROPE].set(inv_freq).reshape(1, LANES)
    cos, sin = _rope_tables(positions.reshape(T, 1), invf, S)
    xf = x.reshape(T, D)
    for l in range(depth):
        lw = _layer_weights(l, *weights)
        sh1, sc1, g1, sh2, sc2, g2 = [m.reshape(B, 1, D) for m in jnp.split(mod[l], 6, axis=-1)]
        q, k, v, conv, sbq, sbk, sbv = _mixer_in(xf, cos, sin, sh1, sc1, lw, B, S, tm_in)
        att = _mla(q, k, v, B, S, tq_mla)
        sbo = _stickbreak(sbq, sbk, sbv, B, S, t_sb)
        x1, h2, cnt, e1, rank, e2 = _mid(att, conv, sbo, xf, g1, sh2, sc2, lw, B, S, tm_mid, alpha)
        xf = _peer(h2, cnt, e1, rank, e2, x1, g2, lw, B, S, tm_peer, rows, alpha)
    return xf.reshape(B, S, D)


def kernel(x, c, positions, ada_w, ada_b, w_in, q_norm_g, kv_norm_g, w_uq, w_ukv, conv_w, w_o,
           ln1_g, ln1_b, peer_wq, peer_k1, peer_k2, peer_u, peer_v, ln2_g, ln2_b):
    return _forward(x, c, positions, ada_w, ada_b, w_in, q_norm_g, kv_norm_g, w_uq, w_ukv, conv_w,
                    w_o, ln1_g, ln1_b, peer_wq, peer_k1, peer_k2, peer_u, peer_v, ln2_g, ln2_b,
                    tm_in=512, tq_mla=256, t_sb=256, tm_mid=256, tm_peer=512, rows=16)
```

```python
import functools
import math

import numpy as np
import jax
import jax.numpy as jnp
from jax import lax
from jax.experimental import pallas as pl
from jax.experimental.pallas import tpu as pltpu

F32 = jnp.float32
BF16 = jnp.bfloat16

D_MODEL = 1024
CHUNK = 64

MLA_HEADS = 8
MLA_NOPE = 64
MLA_ROPE = 32
MLA_V = 64
MLA_Q_RANK = 256
MLA_KV_RANK = 128
ROPE_THETA = 10000.0
HEAD_PAD = 128

CONV_CH = 256
CONV_WIDTH = 3
SB_HEADS = 4
SB_HEAD_DIM = 64
SB_WIDTH = 256

PEER_HEADS = 8
PEER_N_KEYS = 128
PEER_HALF = 128
PEER_TOPK = 16

LN_EPS = 1e-5
RMS_EPS = 1e-6
NEG_INF = -1e30
SQRT_HALF = 0.7071067811865476

LANES = 128
SUBLANES = 8
VMEM_LIMIT = 56 * 1024 * 1024
MXU_TILE_ROWS = 1024
MXU_TILE_LANES = 2 * LANES

_O_QLAT, _O_KVLAT, _O_KR = 0, 256, 384
_O_CB, _O_CC, _O_CH = 512, 768, 1024
_O_SQ, _O_SK, _O_SV = 1280, 1536, 1792
IN_PAD = 2048

_NT = (((1,), (1,)), ((), ()))


def _ln_noaffine(x):
    mu = jnp.mean(x, axis=-1, keepdims=True)
    xc = x - mu
    var = jnp.mean(xc * xc, axis=-1, keepdims=True)
    return xc * lax.rsqrt(var + LN_EPS)


def _rms(x, gain):
    return x * lax.rsqrt(jnp.mean(x * x, axis=-1, keepdims=True) + RMS_EPS) * gain


def _params(*sem):
    return pltpu.CompilerParams(dimension_semantics=sem, vmem_limit_bytes=VMEM_LIMIT)


def _adaln_body(c_ref, w_ref, b_ref, o_ref):
    c = c_ref[...]
    ca = c * jax.nn.sigmoid(c)
    o_ref[0] = jnp.dot(ca, w_ref[0], preferred_element_type=F32,
                       precision=lax.Precision.HIGHEST) + b_ref[0]


def _adaln(c, ada_w, ada_b):
    L, D, N = ada_w.shape
    B = c.shape[0]
    tn = 1024
    return pl.pallas_call(
        _adaln_body,
        grid=(L, N // tn),
        in_specs=[pl.BlockSpec((B, D), lambda l, n: (0, 0)),
                  pl.BlockSpec((1, D, tn), lambda l, n: (l, 0, n)),
                  pl.BlockSpec((1, 1, tn), lambda l, n: (l, 0, n))],
        out_specs=pl.BlockSpec((1, B, tn), lambda l, n: (l, 0, n)),
        out_shape=jax.ShapeDtypeStruct((L, B, N), F32),
        compiler_params=_params("arbitrary", "arbitrary"),
        name="adaln",
    )(c, ada_w, ada_b.reshape(L, 1, N))


def _rope_tables_body(pos_ref, invf_ref, cos_ref, sin_ref):
    ang = pos_ref[...].astype(F32) * invf_ref[...]
    cos_ref[...] = jnp.cos(ang)
    sin_ref[...] = jnp.sin(ang)


def _rope_tables(pos, invf, tm):
    T = pos.shape[0]
    row = lambda i: (i, 0)
    tab = jax.ShapeDtypeStruct((T, LANES), F32)
    return pl.pallas_call(
        _rope_tables_body,
        grid=(T // tm,),
        in_specs=[pl.BlockSpec((tm, 1), row), pl.BlockSpec((1, LANES), lambda i: (0, 0))],
        out_specs=[pl.BlockSpec((tm, LANES), row)] * 2,
        out_shape=[tab, tab],
        compiler_params=_params("arbitrary"),
        name="rope_tables",
    )(pos, invf)


def _mixer_in_body(cos_ref, sin_ref, x_ref, sh_ref, sc_ref, win_ref, qg_ref, kvg_ref, wuq_ref,
                   wuqr_ref, wk2_ref, wk2r_ref, wv_ref, cw_ref,
                   q_ref, k_ref, v_ref, conv_ref, sbq_ref, sbk_ref, sbv_ref, ubuf_ref, *, tm):
    j = pl.program_id(1)
    h = _ln_noaffine(x_ref[...]) * (1.0 + sc_ref[0]) + sh_ref[0]
    proj = jnp.dot(h.astype(BF16), win_ref[...], preferred_element_type=F32)

    qn = _rms(proj[:, _O_QLAT:_O_QLAT + MLA_Q_RANK], qg_ref[...])
    kvn = _rms(proj[:, _O_KVLAT:_O_KVLAT + MLA_KV_RANK], kvg_ref[...])
    kr = proj[:, _O_KR:_O_KR + LANES]
    q = jnp.dot(qn.astype(BF16), wuq_ref[...], preferred_element_type=F32)
    kcat = jnp.concatenate([kvn, kr], axis=-1).astype(BF16)
    k = jnp.dot(kcat, wk2_ref[...], preferred_element_type=F32)
    v = jnp.dot(kvn.astype(BF16), wv_ref[...], preferred_element_type=F32)
    v_ref[...] = v.astype(BF16)

    q_rot = jnp.dot(qn.astype(BF16), wuqr_ref[...], preferred_element_type=F32)
    k_rot = jnp.dot(kcat, wk2r_ref[...], preferred_element_type=F32)
    cos = cos_ref[...]
    sin = sin_ref[...]
    qscale = (MLA_NOPE + MLA_ROPE) ** -0.5
    for hb in range(MLA_HEADS):
        sl = slice(hb * HEAD_PAD, (hb + 1) * HEAD_PAD)
        q_ref[:, sl] = ((q[:, sl] * cos + q_rot[:, sl] * sin) * qscale).astype(BF16)
        k_ref[:, sl] = (k[:, sl] * cos + k_rot[:, sl] * sin).astype(BF16)

    cb = proj[:, _O_CB:_O_CB + CONV_CH]
    u = proj[:, _O_CC:_O_CC + CONV_CH] * proj[:, _O_CH:_O_CH + CONV_CH]

    @pl.when(j == 0)
    def _():
        ubuf_ref[0:SUBLANES, :] = jnp.zeros((SUBLANES, CONV_CH), F32)

    ubuf_ref[SUBLANES:SUBLANES + tm, :] = u
    y = (cw_ref[0:1, :] * ubuf_ref[SUBLANES - 2:SUBLANES - 2 + tm, :]
         + cw_ref[1:2, :] * ubuf_ref[SUBLANES - 1:SUBLANES - 1 + tm, :]
         + cw_ref[2:3, :] * u)
    conv_ref[...] = (cb * y).astype(BF16)
    ubuf_ref[0:SUBLANES, :] = ubuf_ref[tm:tm + SUBLANES, :]

    for hh in range(SB_HEADS):
        o = hh * SB_HEAD_DIM
        sbq_ref[hh] = (proj[:, _O_SQ + o:_O_SQ + o + SB_HEAD_DIM] * (SB_HEAD_DIM ** -0.5)).astype(BF16)
        sbk_ref[hh] = proj[:, _O_SK + o:_O_SK + o + SB_HEAD_DIM].astype(BF16)
        sbv_ref[hh] = proj[:, _O_SV + o:_O_SV + o + SB_HEAD_DIM].astype(BF16)


def _mixer_in(x, cos, sin, sh, sc, lw, B, S, tm):
    T, D = x.shape
    nt = S // tm
    row = lambda b, j: (b * nt + j, 0)
    per_b = lambda b, j: (b, 0, 0)
    const = lambda b, j: (0, 0)
    full = lambda a: pl.BlockSpec(a.shape, const)
    outs = [(MLA_HEADS * HEAD_PAD, BF16), (MLA_HEADS * HEAD_PAD, BF16), (MLA_HEADS * MLA_V, BF16),
            (CONV_CH, BF16)]
    sb_spec = pl.BlockSpec((SB_HEADS, tm, SB_HEAD_DIM), lambda b, j: (0, b * nt + j, 0))
    sb_shape = jax.ShapeDtypeStruct((SB_HEADS, T, SB_HEAD_DIM), BF16)
    return pl.pallas_call(
        functools.partial(_mixer_in_body, tm=tm),
        grid=(B, nt),
        in_specs=[pl.BlockSpec((tm, LANES), row), pl.BlockSpec((tm, LANES), row),
                  pl.BlockSpec((tm, D), row),
                  pl.BlockSpec((1, 1, D), per_b), pl.BlockSpec((1, 1, D), per_b),
                  full(lw["w_in"]), full(lw["qg"]), full(lw["kvg"]), full(lw["w_uq"]),
                  full(lw["w_uq_rot"]), full(lw["wk2"]), full(lw["wk2_rot"]), full(lw["wv"]),
                  full(lw["conv_w"])],
        out_specs=[pl.BlockSpec((tm, n), row) for n, _ in outs] + [sb_spec] * 3,
        out_shape=[jax.ShapeDtypeStruct((T, n), dt) for n, dt in outs] + [sb_shape] * 3,
        scratch_shapes=[pltpu.VMEM((tm + SUBLANES, CONV_CH), F32)],
        compiler_params=_params("arbitrary", "arbitrary"),
        name="mixer_in",
    )(cos, sin, x, sh, sc, lw["w_in"], lw["qg"], lw["kvg"], lw["w_uq"], lw["w_uq_rot"], lw["wk2"],
      lw["wk2_rot"], lw["wv"], lw["conv_w"])


def _mla_body(q_ref, k_ref, v_ref, o_ref, *, S, tq):
    r = lax.broadcasted_iota(jnp.int32, (tq, tq), 0) // CHUNK
    c = lax.broadcasted_iota(jnp.int32, (tq, tq), 1) // CHUNK
    visible = c <= r

    for hh in range(2):
        hs = slice(hh * HEAD_PAD, (hh + 1) * HEAD_PAD)
        vs = slice(hh * MLA_V, (hh + 1) * MLA_V)
        for i in range(S // tq):
            rows = slice(i * tq, (i + 1) * tq)
            q = q_ref[rows, hs]
            s_d = lax.dot_general(q, k_ref[rows, hs], _NT, preferred_element_type=F32)
            s_d = jnp.where(visible, s_d, NEG_INF)
            m = jnp.max(s_d, axis=-1, keepdims=True)
            if i > 0:
                s_o = lax.dot_general(q, k_ref[0:i * tq, hs], _NT, preferred_element_type=F32)
                m = jnp.maximum(m, jnp.max(s_o, axis=-1, keepdims=True))
            p_d = jnp.exp(s_d - m)
            l = jnp.sum(p_d, axis=-1, keepdims=True)
            acc = jnp.dot(p_d.astype(BF16), v_ref[rows, vs], preferred_element_type=F32)
            if i > 0:
                p_o = jnp.exp(s_o - m)
                l = l + jnp.sum(p_o, axis=-1, keepdims=True)
                acc = acc + jnp.dot(p_o.astype(BF16), v_ref[0:i * tq, vs], preferred_element_type=F32)
            o_ref[rows, vs] = (acc / l).astype(BF16)


def _mla(q, k, v, B, S, tq):
    T = q.shape[0]
    hp = MLA_HEADS // 2
    return pl.pallas_call(
        functools.partial(_mla_body, S=S, tq=tq),
        grid=(B, hp),
        in_specs=[pl.BlockSpec((S, 2 * HEAD_PAD), lambda b, h: (b, h)),
                  pl.BlockSpec((S, 2 * HEAD_PAD), lambda b, h: (b, h)),
                  pl.BlockSpec((S, 2 * MLA_V), lambda b, h: (b, h))],
        out_specs=pl.BlockSpec((S, 2 * MLA_V), lambda b, h: (b, h)),
        out_shape=jax.ShapeDtypeStruct((T, MLA_HEADS * MLA_V), BF16),
        compiler_params=_params("arbitrary", "arbitrary"),
        name="mla",
    )(q, k, v)


def _softplus(z):
    return jnp.maximum(z, 0.0) + jnp.log(1.0 + jnp.exp(-jnp.abs(z)))


def _stickbreak_body(q_ref, k_ref, v_ref, o_ref, *, S, t):
    ri = lax.broadcasted_iota(jnp.int32, (t, t), 0)
    ci = lax.broadcasted_iota(jnp.int32, (t, t), 1)
    strict = ci < ri
    upper = jnp.where(ri >= ci, 1.0, 0.0).astype(BF16)

    for i in range(S // t):
        rows = slice(i * t, (i + 1) * t)
        z = lax.dot_general(q_ref[0, rows, :], k_ref[0, 0:(i + 1) * t, :], _NT,
                            preferred_element_type=F32)
        sp = _softplus(z)
        after = jnp.zeros((t, 1), F32)
        ws = [None] * (i + 1)
        for j in range(i, -1, -1):
            cols = slice(j * t, (j + 1) * t)
            sp_j = sp[:, cols]
            if j == i:
                sp_j = jnp.where(strict, sp_j, 0.0)
            hi = sp_j.astype(BF16)
            lo = (sp_j - hi.astype(F32)).astype(BF16)
            incl = (jnp.dot(hi, upper, preferred_element_type=F32)
                    + jnp.dot(lo, upper, preferred_element_type=F32))
            w = jnp.exp(z[:, cols] - incl - after)
            if j == i:
                w = jnp.where(strict, w, 0.0)
            ws[j] = w.astype(BF16)
            after = after + incl[:, 0:1]
        w_all = ws[0] if i == 0 else jnp.concatenate(ws, axis=1)
        o = jnp.dot(w_all, v_ref[0, 0:(i + 1) * t, :], preferred_element_type=F32)
        o_ref[0, rows, :] = o.astype(BF16)


def _stickbreak(q, k, v, B, S, t):
    T = q.shape[1]
    spec = pl.BlockSpec((1, S, SB_HEAD_DIM), lambda b, h: (h, b, 0))
    return pl.pallas_call(
        functools.partial(_stickbreak_body, S=S, t=t),
        grid=(B, SB_HEADS),
        in_specs=[spec, spec, spec],
        out_specs=spec,
        out_shape=jax.ShapeDtypeStruct((SB_HEADS, T, SB_HEAD_DIM), BF16),
        compiler_params=_params("arbitrary", "arbitrary"),
        name="stickbreak",
    )(q, k, v)


def _oddeven_pairs(n):
    pairs = []
    p = 1
    while p < n:
        k = p
        while k >= 1:
            for j in range(k % p, n - k, 2 * k):
                for i in range(min(k, n - j - k)):
                    if (i + j) // (2 * p) == (i + j + k) // (2 * p):
                        pairs.append((i + j, i + j + k))
            k //= 2
        p *= 2
    return pairs


_SORT16 = _oddeven_pairs(PEER_TOPK)


def _cmpx(xs, i, j):
    a, b = xs[i], xs[j]
    if a is None:
        xs[i], xs[j] = b, None
    elif b is not None:
        xs[i], xs[j] = jnp.maximum(a, b), jnp.minimum(a, b)


def _bitonic_merge(xs):
    n = len(xs)
    d = n // 2
    while d >= 1:
        for i in range(n):
            if (i // d) % 2 == 0:
                _cmpx(xs, i, i + d)
        d //= 2


def _top16_sorted(xs):
    xs = list(xs)
    for i, j in _SORT16:
        _cmpx(xs, i, j)
    for shift in (4, 2, 1):
        ps = [None if x is None else pltpu.roll(x, shift, 0) for x in xs]
        merged = []
        for i in range(PEER_TOPK):
            a, b = xs[i], ps[PEER_TOPK - 1 - i]
            merged.append(b if a is None else a if b is None else jnp.maximum(a, b))
        xs = merged
        _bitonic_merge(xs)
    return xs


def _top16_pair_sums(v1, v2):
    sub = lax.broadcasted_iota(jnp.int32, v1[0].shape, 0)

    def pack(vals):
        out = vals[0]
        for s in range(1, SUBLANES):
            out = jnp.where(sub == s, vals[s], out)
        return out

    b_lo, b_hi = pack(v2[:SUBLANES]), pack(v2[SUBLANES:])
    cands = [v1[0] + b_lo, v1[0] + b_hi]
    for a in range(1, SUBLANES):
        s = v1[a] + b_lo
        lim = PEER_TOPK // (a + 1)
        cands.append(s if lim >= SUBLANES else jnp.where(sub < lim, s, -jnp.inf))
    cands.append(pack(v1[SUBLANES:]) + v2[0])
    cands += [None] * (PEER_TOPK - len(cands))
    return _top16_sorted(cands)


def _count_prefix(pred, vals):
    n = len(vals)
    bits = []

    def pivot(level_vals, k):
        if k == len(bits):
            return level_vals[0]
        half = len(level_vals) // 2
        return jnp.where(bits[k], pivot(level_vals[half:], k + 1), pivot(level_vals[:half], k + 1))

    step = n // 2
    while step >= 1:
        cands = [vals[base + step - 1] for base in range(0, n, 2 * step)]
        bits.append(pred(pivot(cands, 0)))
        step //= 2
    count = jnp.where(bits[0], float(n // 2), 0.0)
    for k, b in enumerate(bits[1:], start=1):
        count = count + jnp.where(b, float(n >> (k + 1)), 0.0)
    return jnp.where(pred(vals[n - 1]), float(n), count)


def _mid_body(att_ref, conv_ref, sb_ref, x_ref, g1_ref, sh2_ref, sc2_ref, lng_ref, lnb_ref,
              wo_ref, wq_ref, k1_ref, k2_ref,
              x1_ref, h2_ref, cnt_ref, e1_ref, rank_ref, e2_ref, *, alpha):
    cat = jnp.concatenate([att_ref[...], conv_ref[...]] + [sb_ref[h] for h in range(SB_HEADS)], axis=-1)
    mix = jnp.dot(cat, wo_ref[...], preferred_element_type=F32)
    x1 = _ln_noaffine(alpha * x_ref[...] + (1.0 + g1_ref[0]) * mix) * lng_ref[...] + lnb_ref[...]
    x1_ref[...] = x1
    h2 = (_ln_noaffine(x1) * (1.0 + sc2_ref[0]) + sh2_ref[0]).astype(BF16)
    h2_ref[...] = h2
    qp = jnp.dot(h2, wq_ref[...], preferred_element_type=F32).astype(BF16)
    _route(qp, k1_ref, k2_ref, cnt_ref, e1_ref, rank_ref, e2_ref)


def _route(qp, k1_ref, k2_ref, cnt_ref, e1_ref, rank_ref, e2_ref):
    for h in range(PEER_HEADS):
        o = h * 2 * PEER_HALF
        s1 = lax.dot_general(k1_ref[h], qp[:, o:o + PEER_HALF], _NT, preferred_element_type=F32)
        s2 = lax.dot_general(k2_ref[h], qp[:, o + PEER_HALF:o + 2 * PEER_HALF], _NT,
                             preferred_element_type=F32)
        split = lambda s: [s[SUBLANES * i:SUBLANES * (i + 1), :] for i in range(PEER_N_KEYS // SUBLANES)]
        v1 = _top16_sorted(split(s1))
        v2 = _top16_sorted(split(s2))
        top = _top16_pair_sums(v1, v2)
        tau = top[PEER_TOPK - 1][0:1, :]
        z = jnp.ones_like(top[0])
        for t in top[1:]:
            z = z + jnp.exp(t - top[0])
        e1_ref[h] = jnp.exp(s1 - v1[0][0:1, :]).astype(BF16)
        e2_ref[h] = (jnp.exp(s2 - v2[0][0:1, :]) * (0.5 / z[0:1, :])).astype(BF16)
        col = [v[0:1, :] for v in v2]
        cnt_ref[h] = _count_prefix(lambda vb: s1 + vb >= tau, col).astype(BF16)
        rank_ref[h] = _count_prefix(lambda vb: s2 < vb, col).astype(BF16)


def _mid(att, conv, sb, x, g1, sh2, sc2, lw, B, S, tm, alpha):
    T, D = x.shape
    nt = S // tm
    row = lambda i: (i, 0)
    per_b = lambda i: (i // nt, 0, 0)
    const2 = lambda i: (0, 0)
    const3 = lambda i: (0, 0, 0)
    route = pl.BlockSpec((PEER_HEADS, PEER_N_KEYS, tm), lambda i: (0, 0, i))
    route_shape = jax.ShapeDtypeStruct((PEER_HEADS, PEER_N_KEYS, T), BF16)
    return pl.pallas_call(
        functools.partial(_mid_body, alpha=alpha),
        grid=(T // tm,),
        in_specs=[pl.BlockSpec((tm, att.shape[1]), row), pl.BlockSpec((tm, CONV_CH), row),
                  pl.BlockSpec((SB_HEADS, tm, SB_HEAD_DIM), lambda i: (0, i, 0)),
                  pl.BlockSpec((tm, D), row),
                  pl.BlockSpec((1, 1, D), per_b), pl.BlockSpec((1, 1, D), per_b),
                  pl.BlockSpec((1, 1, D), per_b),
                  pl.BlockSpec((1, D), const2), pl.BlockSpec((1, D), const2),
                  pl.BlockSpec(lw["w_o"].shape, const2), pl.BlockSpec(lw["wq"].shape, const2),
                  pl.BlockSpec(lw["k1"].shape, const3), pl.BlockSpec(lw["k2"].shape, const3)],
        out_specs=[pl.BlockSpec((tm, D), row), pl.BlockSpec((tm, D), row), route, route, route, route],
        out_shape=[jax.ShapeDtypeStruct((T, D), F32), jax.ShapeDtypeStruct((T, D), BF16),
                   route_shape, route_shape, route_shape, route_shape],
        compiler_params=_params("arbitrary"),
        name="mid",
    )(att, conv, sb, x, g1, sh2, sc2, lw["ln1_g"], lw["ln1_b"], lw["w_o"], lw["wq"],
      lw["k1"], lw["k2"])


def _peer_body(h2_ref, u_ref, vt_ref, cnt_ref, e1_ref, rank_ref, e2_ref, x1_ref, g2_ref, lng_ref,
               lnb_ref, o_ref, a_ref, w_ref, acc_ref, *, tm, rows, alpha):
    j = pl.program_id(1)

    @pl.when(j == 0)
    def _():
        acc_ref[...] = jnp.zeros_like(acc_ref)

    for e0 in range(0, rows * PEER_N_KEYS, MXU_TILE_ROWS):
        for t0 in range(0, tm, MXU_TILE_LANES):
            a_ref[e0:e0 + MXU_TILE_ROWS, t0:t0 + MXU_TILE_LANES] = lax.dot_general(
                u_ref[e0:e0 + MXU_TILE_ROWS, :], h2_ref[t0:t0 + MXU_TILE_LANES, :], _NT,
                preferred_element_type=F32).astype(BF16)

    for rr in range(rows):
        rs = slice(rr * PEER_N_KEYS, (rr + 1) * PEER_N_KEYS)
        tw = 2 * LANES
        for ts in range(tm // tw):
            ln = slice(ts * tw, (ts + 1) * tw)
            g = jnp.zeros((PEER_N_KEYS, tw), BF16)
            for h in range(PEER_HEADS):
                cnt = cnt_ref[h, rr:rr + 1, ln]
                e1 = e1_ref[h, rr:rr + 1, ln]
                e2 = e2_ref[h, :, ln]
                g = g + jnp.where(rank_ref[h, :, ln] < cnt, e2, jnp.zeros_like(e2)) * e1
            a = a_ref[rs, ln]
            act = a * (1.0 + lax.erf(a * SQRT_HALF))
            w_ref[rs, ln] = act * g
    acc_ref[...] += jnp.dot(vt_ref[...], w_ref[...], preferred_element_type=F32)

    @pl.when(j == pl.num_programs(1) - 1)
    def _():
        ffn = acc_ref[...].T
        y = alpha * x1_ref[...] + (1.0 + g2_ref[0]) * ffn
        o_ref[...] = _ln_noaffine(y) * lng_ref[...] + lnb_ref[...]


def _peer(h2, cnt, e1, rank, e2, x1, g2, lw, B, S, tm, rows, alpha):
    T, D = x1.shape
    nt = S // tm
    ne = rows * PEER_N_KEYS
    n_steps = PEER_N_KEYS // rows
    row = lambda i, j: (i, 0)
    const2 = lambda i, j: (0, 0)
    return pl.pallas_call(
        functools.partial(_peer_body, tm=tm, rows=rows, alpha=alpha),
        grid=(T // tm, n_steps),
        in_specs=[pl.BlockSpec((tm, D), row),
                  pl.BlockSpec((ne, D), lambda i, j: (j, 0)),
                  pl.BlockSpec((D, ne), lambda i, j: (0, j)),
                  pl.BlockSpec((PEER_HEADS, rows, tm), lambda i, j: (0, j, i)),
                  pl.BlockSpec((PEER_HEADS, rows, tm), lambda i, j: (0, j, i)),
                  pl.BlockSpec((PEER_HEADS, PEER_N_KEYS, tm), lambda i, j: (0, 0, i)),
                  pl.BlockSpec((PEER_HEADS, PEER_N_KEYS, tm), lambda i, j: (0, 0, i)),
                  pl.BlockSpec((tm, D), row),
                  pl.BlockSpec((1, 1, D), lambda i, j: (i // nt, 0, 0)),
                  pl.BlockSpec((1, D), const2), pl.BlockSpec((1, D), const2)],
        out_specs=pl.BlockSpec((tm, D), row),
        out_shape=jax.ShapeDtypeStruct((T, D), F32),
        scratch_shapes=[pltpu.VMEM((ne, tm), BF16), pltpu.VMEM((ne, tm), BF16),
                        pltpu.VMEM((D, tm), F32)],
        compiler_params=_params("arbitrary", "arbitrary"),
        name="peer",
    )(h2, lw["u"], lw["vt"], cnt, e1, rank, e2, x1, g2, lw["ln2_g"], lw["ln2_b"])


def _layer_weights(l, w_in, q_norm_g, kv_norm_g, w_uq, w_ukv, conv_w, w_o, ln1_g, ln1_b,
                   peer_wq, peer_k1, peer_k2, peer_u, peer_v, ln2_g, ln2_b):
    D = D_MODEL
    wi = w_in[l]
    pad = jnp.zeros((D, LANES - MLA_ROPE), F32)
    w_in_p = jnp.concatenate([wi[:, :416], pad, wi[:, 416:]], axis=1).astype(BF16)
    qk = MLA_NOPE + MLA_ROPE
    w_uq_p = jnp.pad(w_uq[l].reshape(MLA_Q_RANK, MLA_HEADS, qk),
                     ((0, 0), (0, 0), (0, HEAD_PAD - qk))).reshape(MLA_Q_RANK, -1).astype(BF16)
    ukv = w_ukv[l].reshape(MLA_KV_RANK, MLA_HEADS, MLA_NOPE + MLA_V)
    wk = jnp.pad(ukv[:, :, :MLA_NOPE], ((0, 0), (0, 0), (0, HEAD_PAD - MLA_NOPE)))
    expand = np.zeros((LANES, MLA_HEADS, HEAD_PAD), np.float32)
    for r in range(MLA_ROPE):
        expand[r, :, MLA_NOPE + r] = 1.0
    wk2 = jnp.concatenate([wk, jnp.asarray(expand)], axis=0).reshape(MLA_KV_RANK + LANES, -1).astype(BF16)
    wv = ukv[:, :, MLA_NOPE:].reshape(MLA_KV_RANK, -1).astype(BF16)

    def rotary_partner(w):
        w3 = w.reshape(w.shape[0], MLA_HEADS, HEAD_PAD)
        half = MLA_ROPE // 2
        a, b, c = MLA_NOPE, MLA_NOPE + half, MLA_NOPE + MLA_ROPE
        out = jnp.zeros_like(w3).at[:, :, a:b].set(-w3[:, :, b:c]).at[:, :, b:c].set(w3[:, :, a:b])
        return out.reshape(w.shape)
    return dict(
        w_in=w_in_p, qg=q_norm_g[l].reshape(1, -1), kvg=kv_norm_g[l].reshape(1, -1),
        w_uq=w_uq_p, w_uq_rot=rotary_partner(w_uq_p), wk2=wk2, wk2_rot=rotary_partner(wk2), wv=wv,
        conv_w=conv_w[l],
        w_o=w_o[l].astype(BF16), ln1_g=ln1_g[l].reshape(1, D), ln1_b=ln1_b[l].reshape(1, D),
        wq=peer_wq[l].astype(BF16), k1=peer_k1[l].astype(BF16), k2=peer_k2[l].astype(BF16),
        u=peer_u[l].astype(BF16), vt=peer_v[l].T.astype(BF16),
        ln2_g=ln2_g[l].reshape(1, D), ln2_b=ln2_b[l].reshape(1, D))


def _forward(x, c, positions, ada_w, ada_b, *weights, tm_in, tq_mla, t_sb, tm_mid, tm_peer, rows):
    B, S, D = x.shape
    T = B * S
    depth = ada_w.shape[0]
    alpha = (2.0 * depth) ** 0.25
    mod = _adaln(c, ada_w, ada_b)
    half = MLA_ROPE // 2
    inv_freq = ROPE_THETA ** (-jnp.arange(half, dtype=F32) / half)
    invf = jnp.zeros((LANES,), F32).at[MLA_NOPE:MLA_NOPE + half].set(inv_freq)
    invf = invf.at[MLA_NOPE + half:MLA_NOPE + MLA_ROPE].set(inv_freq).reshape(1, LANES)
    cos, sin = _rope_tables(positions.reshape(T, 1), invf, S)
    xf = x.reshape(T, D)
    for l in range(depth):
        lw = _layer_weights(l, *weights)
        sh1, sc1, g1, sh2, sc2, g2 = [m.reshape(B, 1, D) for m in jnp.split(mod[l], 6, axis=-1)]
        q, k, v, conv, sbq, sbk, sbv = _mixer_in(xf, cos, sin, sh1, sc1, lw, B, S, tm_in)
        att = _mla(q, k, v, B, S, tq_mla)
        sbo = _stickbreak(sbq, sbk, sbv, B, S, t_sb)
        x1, h2, cnt, e1, rank, e2 = _mid(att, conv, sbo, xf, g1, sh2, sc2, lw, B, S, tm_mid, alpha)
        xf = _peer(h2, cnt, e1, rank, e2, x1, g2, lw, B, S, tm_peer, rows, alpha)
    return xf.reshape(B, S, D)


def kernel(x, c, positions, ada_w, ada_b, w_in, q_norm_g, kv_norm_g, w_uq, w_ukv, conv_w, w_o,
           ln1_g, ln1_b, peer_wq, peer_k1, peer_k2, peer_u, peer_v, ln2_g, ln2_b):
    return _forward(x, c, positions, ada_w, ada_b, w_in, q_norm_g, kv_norm_g, w_uq, w_ukv, conv_w,
                    w_o, ln1_g, ln1_b, peer_wq, peer_k1, peer_k2, peer_u, peer_v, ln2_g, ln2_b,
                    tm_in=512, tq_mla=512, t_sb=256, tm_mid=256, tm_peer=512, rows=16)
```

```python
import functools

import numpy as np
import jax
import jax.numpy as jnp
from jax import lax
from jax.experimental import pallas as pl
from jax.experimental.pallas import tpu as pltpu

F32 = jnp.float32
BF16 = jnp.bfloat16

D_MODEL = 1024
CHUNK = 64

MLA_HEADS = 8
MLA_NOPE = 64
MLA_ROPE = 32
MLA_V = 64
MLA_Q_RANK = 256
MLA_KV_RANK = 128
ROPE_THETA = 10000.0
HEAD_PAD = 128

CONV_CH = 256
SB_HEADS = 4
SB_HEAD_DIM = 64
SB_WIDTH = SB_HEADS * SB_HEAD_DIM

PEER_HEADS = 8
PEER_N_KEYS = 128
PEER_HALF = 128
PEER_TOPK = 16

LN_EPS = 1e-5
RMS_EPS = 1e-6
NEG_INF = -1e30
SQRT_HALF = 0.7071067811865476

LANES = 128
SUBLANES = 8
VMEM_LIMIT = 56 * 1024 * 1024
MXU_TILE_ROWS = 1024
MXU_TILE_LANES = 2 * LANES

_O_QLAT, _O_KVLAT, _O_KR = 0, 256, 384
_O_CB, _O_CC, _O_CH = 512, 768, 1024
_O_SQ, _O_SK, _O_SV = 1280, 1536, 1792
IN_PAD = 2048

_NT = (((1,), (1,)), ((), ()))


def _ln_noaffine(x):
    mu = jnp.mean(x, axis=-1, keepdims=True)
    xc = x - mu
    var = jnp.mean(xc * xc, axis=-1, keepdims=True)
    return xc * lax.rsqrt(var + LN_EPS)


def _rms(x, gain):
    return x * lax.rsqrt(jnp.mean(x * x, axis=-1, keepdims=True) + RMS_EPS) * gain


def _params(*sem):
    return pltpu.CompilerParams(dimension_semantics=sem, vmem_limit_bytes=VMEM_LIMIT)


def _adaln_body(c_ref, w_ref, b_ref, o_ref):
    c = c_ref[...]
    ca = c * jax.nn.sigmoid(c)
    o_ref[0] = jnp.dot(ca, w_ref[0], preferred_element_type=F32,
                       precision=lax.Precision.HIGHEST) + b_ref[0]


def _adaln(c, ada_w, ada_b):
    L, D, N = ada_w.shape
    B = c.shape[0]
    tn = D
    return pl.pallas_call(
        _adaln_body,
        grid=(L, N // tn),
        in_specs=[pl.BlockSpec((B, D), lambda l, n: (0, 0)),
                  pl.BlockSpec((1, D, tn), lambda l, n: (l, 0, n)),
                  pl.BlockSpec((1, 1, tn), lambda l, n: (l, 0, n))],
        out_specs=pl.BlockSpec((1, B, tn), lambda l, n: (l, 0, n)),
        out_shape=jax.ShapeDtypeStruct((L, B, N), F32),
        compiler_params=_params("arbitrary", "arbitrary"),
        name="adaln",
    )(c, ada_w, ada_b.reshape(L, 1, N))


def _rope_tables_body(pos_ref, invf_ref, cos_ref, sin_ref):
    ang = pos_ref[...].astype(F32) * invf_ref[...]
    cos_ref[...] = jnp.cos(ang)
    sin_ref[...] = jnp.sin(ang)


def _rope_tables(pos, invf, tm):
    T = pos.shape[0]
    row = lambda i: (i, 0)
    tab = jax.ShapeDtypeStruct((T, LANES), F32)
    return pl.pallas_call(
        _rope_tables_body,
        grid=(T // tm,),
        in_specs=[pl.BlockSpec((tm, 1), row), pl.BlockSpec((1, LANES), lambda i: (0, 0))],
        out_specs=[pl.BlockSpec((tm, LANES), row)] * 2,
        out_shape=[tab, tab],
        compiler_params=_params("arbitrary"),
        name="rope_tables",
    )(pos, invf)


def _mixer_in_body(cos_ref, sin_ref, x_ref, sh_ref, sc_ref, win_ref, qg_ref, kvg_ref, wuq_ref,
                   wuqr_ref, wk2_ref, wk2r_ref, wv_ref, cw_ref,
                   q_ref, k_ref, v_ref, conv_ref, sbq_ref, sbk_ref, sbv_ref, ubuf_ref, *, tm):
    j = pl.program_id(1)
    h = _ln_noaffine(x_ref[...]) * (1.0 + sc_ref[0]) + sh_ref[0]
    proj = jnp.dot(h.astype(BF16), win_ref[...], preferred_element_type=F32)

    qn = _rms(proj[:, _O_QLAT:_O_QLAT + MLA_Q_RANK], qg_ref[...])
    kvn = _rms(proj[:, _O_KVLAT:_O_KVLAT + MLA_KV_RANK], kvg_ref[...])
    kr = proj[:, _O_KR:_O_KR + LANES]
    q = jnp.dot(qn.astype(BF16), wuq_ref[...], preferred_element_type=F32)
    kcat = jnp.concatenate([kvn, kr], axis=-1).astype(BF16)
    k = jnp.dot(kcat, wk2_ref[...], preferred_element_type=F32)
    v = jnp.dot(kvn.astype(BF16), wv_ref[...], preferred_element_type=F32)
    v_ref[...] = v.astype(BF16)

    q_rot = jnp.dot(qn.astype(BF16), wuqr_ref[...], preferred_element_type=F32)
    k_rot = jnp.dot(kcat, wk2r_ref[...], preferred_element_type=F32)
    cos = cos_ref[...]
    sin = sin_ref[...]
    qscale = (MLA_NOPE + MLA_ROPE) ** -0.5
    for hb in range(MLA_HEADS):
        sl = slice(hb * HEAD_PAD, (hb + 1) * HEAD_PAD)
        q_ref[:, sl] = ((q[:, sl] * cos + q_rot[:, sl] * sin) * qscale).astype(BF16)
        k_ref[:, sl] = (k[:, sl] * cos + k_rot[:, sl] * sin).astype(BF16)

    cb = proj[:, _O_CB:_O_CB + CONV_CH]
    u = proj[:, _O_CC:_O_CC + CONV_CH] * proj[:, _O_CH:_O_CH + CONV_CH]

    @pl.when(j == 0)
    def _():
        ubuf_ref[0:SUBLANES, :] = jnp.zeros((SUBLANES, CONV_CH), F32)

    ubuf_ref[SUBLANES:SUBLANES + tm, :] = u
    y = (cw_ref[0:1, :] * ubuf_ref[SUBLANES - 2:SUBLANES - 2 + tm, :]
         + cw_ref[1:2, :] * ubuf_ref[SUBLANES - 1:SUBLANES - 1 + tm, :]
         + cw_ref[2:3, :] * u)
    conv_ref[...] = (cb * y).astype(BF16)
    ubuf_ref[0:SUBLANES, :] = ubuf_ref[tm:tm + SUBLANES, :]

    for hh in range(SB_HEADS):
        o = hh * SB_HEAD_DIM
        sbq_ref[hh] = (proj[:, _O_SQ + o:_O_SQ + o + SB_HEAD_DIM] * (SB_HEAD_DIM ** -0.5)).astype(BF16)
        sbk_ref[hh] = proj[:, _O_SK + o:_O_SK + o + SB_HEAD_DIM].astype(BF16)
        sbv_ref[hh] = proj[:, _O_SV + o:_O_SV + o + SB_HEAD_DIM].astype(BF16)


def _mixer_in(x, cos, sin, sh, sc, lw, B, S, tm):
    T, D = x.shape
    nt = S // tm
    row = lambda b, j: (b * nt + j, 0)
    per_b = lambda b, j: (b, 0, 0)
    const = lambda b, j: (0, 0)
    full = lambda a: pl.BlockSpec(a.shape, const)
    outs = [(MLA_HEADS * HEAD_PAD, BF16), (MLA_HEADS * HEAD_PAD, BF16), (MLA_HEADS * MLA_V, BF16),
            (CONV_CH, BF16)]
    sb_spec = pl.BlockSpec((SB_HEADS, tm, SB_HEAD_DIM), lambda b, j: (0, b * nt + j, 0))
    sb_shape = jax.ShapeDtypeStruct((SB_HEADS, T, SB_HEAD_DIM), BF16)
    return pl.pallas_call(
        functools.partial(_mixer_in_body, tm=tm),
        grid=(B, nt),
        in_specs=[pl.BlockSpec((tm, LANES), row), pl.BlockSpec((tm, LANES), row),
                  pl.BlockSpec((tm, D), row),
                  pl.BlockSpec((1, 1, D), per_b), pl.BlockSpec((1, 1, D), per_b),
                  full(lw["w_in"]), full(lw["qg"]), full(lw["kvg"]), full(lw["w_uq"]),
                  full(lw["w_uq_rot"]), full(lw["wk2"]), full(lw["wk2_rot"]), full(lw["wv"]),
                  full(lw["conv_w"])],
        out_specs=[pl.BlockSpec((tm, n), row) for n, _ in outs] + [sb_spec] * 3,
        out_shape=[jax.ShapeDtypeStruct((T, n), dt) for n, dt in outs] + [sb_shape] * 3,
        scratch_shapes=[pltpu.VMEM((tm + SUBLANES, CONV_CH), F32)],
        compiler_params=_params("arbitrary", "arbitrary"),
        name="mixer_in",
    )(cos, sin, x, sh, sc, lw["w_in"], lw["qg"], lw["kvg"], lw["w_uq"], lw["w_uq_rot"], lw["wk2"],
      lw["wk2_rot"], lw["wv"], lw["conv_w"])


def _mla_body(q_ref, k_ref, v_ref, o_ref, *, S, tq):
    r = lax.broadcasted_iota(jnp.int32, (tq, tq), 0) // CHUNK
    c = lax.broadcasted_iota(jnp.int32, (tq, tq), 1) // CHUNK
    visible = c <= r

    for hh in range(2):
        hs = slice(hh * HEAD_PAD, (hh + 1) * HEAD_PAD)
        vs = slice(hh * MLA_V, (hh + 1) * MLA_V)
        for i in range(S // tq):
            rows = slice(i * tq, (i + 1) * tq)
            q = q_ref[rows, hs]
            s_d = lax.dot_general(q, k_ref[rows, hs], _NT, preferred_element_type=F32)
            s_d = jnp.where(visible, s_d, NEG_INF)
            m = jnp.max(s_d, axis=-1, keepdims=True)
            if i > 0:
                s_o = lax.dot_general(q, k_ref[0:i * tq, hs], _NT, preferred_element_type=F32)
                m = jnp.maximum(m, jnp.max(s_o, axis=-1, keepdims=True))
            p_d = jnp.exp(s_d - m)
            l = jnp.sum(p_d, axis=-1, keepdims=True)
            acc = jnp.dot(p_d.astype(BF16), v_ref[rows, vs], preferred_element_type=F32)
            if i > 0:
                p_o = jnp.exp(s_o - m)
                l = l + jnp.sum(p_o, axis=-1, keepdims=True)
                acc = acc + jnp.dot(p_o.astype(BF16), v_ref[0:i * tq, vs], preferred_element_type=F32)
            o_ref[rows, vs] = (acc / l).astype(BF16)


def _mla(q, k, v, B, S, tq):
    T = q.shape[0]
    hp = MLA_HEADS // 2
    return pl.pallas_call(
        functools.partial(_mla_body, S=S, tq=tq),
        grid=(B, hp),
        in_specs=[pl.BlockSpec((S, 2 * HEAD_PAD), lambda b, h: (b, h)),
                  pl.BlockSpec((S, 2 * HEAD_PAD), lambda b, h: (b, h)),
                  pl.BlockSpec((S, 2 * MLA_V), lambda b, h: (b, h))],
        out_specs=pl.BlockSpec((S, 2 * MLA_V), lambda b, h: (b, h)),
        out_shape=jax.ShapeDtypeStruct((T, MLA_HEADS * MLA_V), BF16),
        compiler_params=_params("arbitrary", "arbitrary"),
        name="mla",
    )(q, k, v)


def _softplus(z):
    return jnp.maximum(z, 0.0) + jnp.log(1.0 + jnp.exp(-jnp.abs(z)))


def _stickbreak_body(q_ref, k_ref, v_ref, o_ref, *, S, t):
    ri = lax.broadcasted_iota(jnp.int32, (t, t), 0)
    ci = lax.broadcasted_iota(jnp.int32, (t, t), 1)
    strict = ci < ri
    upper = jnp.where(ri >= ci, 1.0, 0.0).astype(BF16)
    upper2 = jnp.concatenate([upper, upper], axis=0)

    for i in range(S // t):
        rows = slice(i * t, (i + 1) * t)
        z = lax.dot_general(q_ref[0, rows, :], k_ref[0, 0:(i + 1) * t, :], _NT,
                            preferred_element_type=F32)
        sp = _softplus(z)
        after = jnp.zeros((t, 1), F32)
        ws = [None] * (i + 1)
        for j in range(i, -1, -1):
            cols = slice(j * t, (j + 1) * t)
            sp_j = sp[:, cols]
            if j == i:
                sp_j = jnp.where(strict, sp_j, 0.0)
            hi = sp_j.astype(BF16)
            lo = (sp_j - hi.astype(F32)).astype(BF16)
            incl = jnp.dot(jnp.concatenate([hi, lo], axis=1), upper2, preferred_element_type=F32)
            w = jnp.exp(z[:, cols] - incl - after)
            if j == i:
                w = jnp.where(strict, w, 0.0)
            ws[j] = w.astype(BF16)
            after = after + incl[:, 0:1]
        w_all = ws[0] if i == 0 else jnp.concatenate(ws, axis=1)
        o = jnp.dot(w_all, v_ref[0, 0:(i + 1) * t, :], preferred_element_type=F32)
        o_ref[0, rows, :] = o.astype(BF16)


def _stickbreak(q, k, v, B, S, t):
    T = q.shape[1]
    spec = pl.BlockSpec((1, S, SB_HEAD_DIM), lambda b, h: (h, b, 0))
    return pl.pallas_call(
        functools.partial(_stickbreak_body, S=S, t=t),
        grid=(B, SB_HEADS),
        in_specs=[spec, spec, spec],
        out_specs=spec,
        out_shape=jax.ShapeDtypeStruct((SB_HEADS, T, SB_HEAD_DIM), BF16),
        compiler_params=_params("arbitrary", "arbitrary"),
        name="stickbreak",
    )(q, k, v)


def _oddeven_pairs(n):
    pairs = []
    p = 1
    while p < n:
        k = p
        while k >= 1:
            for j in range(k % p, n - k, 2 * k):
                for i in range(min(k, n - j - k)):
                    if (i + j) // (2 * p) == (i + j + k) // (2 * p):
                        pairs.append((i + j, i + j + k))
            k //= 2
        p *= 2
    return pairs


_SORT16 = _oddeven_pairs(PEER_TOPK)


def _cmpx(xs, i, j):
    a, b = xs[i], xs[j]
    if a is None:
        xs[i], xs[j] = b, None
    elif b is not None:
        xs[i], xs[j] = jnp.maximum(a, b), jnp.minimum(a, b)


def _bitonic_merge(xs):
    n = len(xs)
    d = n // 2
    while d >= 1:
        for i in range(n):
            if (i // d) % 2 == 0:
                _cmpx(xs, i, i + d)
        d //= 2


def _top16_sorted(xs):
    xs = list(xs)
    for i, j in _SORT16:
        _cmpx(xs, i, j)
    for shift in (4, 2, 1):
        ps = [None if x is None else pltpu.roll(x, shift, 0) for x in xs]
        merged = []
        for i in range(PEER_TOPK):
            a, b = xs[i], ps[PEER_TOPK - 1 - i]
            merged.append(b if a is None else a if b is None else jnp.maximum(a, b))
        xs = merged
        _bitonic_merge(xs)
    return xs


def _top16_pair_sums(v1, v2):
    sub = lax.broadcasted_iota(jnp.int32, v1[0].shape, 0)

    def pack(vals):
        out = vals[0]
        for s in range(1, SUBLANES):
            out = jnp.where(sub == s, vals[s], out)
        return out

    b_lo, b_hi = pack(v2[:SUBLANES]), pack(v2[SUBLANES:])
    cands = [v1[0] + b_lo, v1[0] + b_hi]
    for a in range(1, SUBLANES):
        s = v1[a] + b_lo
        lim = PEER_TOPK // (a + 1)
        cands.append(s if lim >= SUBLANES else jnp.where(sub < lim, s, -jnp.inf))
    cands.append(pack(v1[SUBLANES:]) + v2[0])
    cands += [None] * (PEER_TOPK - len(cands))
    return _top16_sorted(cands)


def _count_prefix(pred, vals):
    n = len(vals)
    bits = []

    def pivot(level_vals, k):
        if k == len(bits):
            return level_vals[0]
        half = len(level_vals) // 2
        return jnp.where(bits[k], pivot(level_vals[half:], k + 1), pivot(level_vals[:half], k + 1))

    step = n // 2
    while step >= 1:
        cands = [vals[base + step - 1] for base in range(0, n, 2 * step)]
        bits.append(pred(pivot(cands, 0)))
        step //= 2
    count = jnp.where(bits[0], float(n // 2), 0.0)
    for k, b in enumerate(bits[1:], start=1):
        count = count + jnp.where(b, float(n >> (k + 1)), 0.0)
    return jnp.where(pred(vals[n - 1]), float(n), count)


def _mid_body(att_ref, conv_ref, sb_ref, x_ref, g1_ref, sh2_ref, sc2_ref, lng_ref, lnb_ref,
              wo_ref, wq_ref, k1_ref, k2_ref,
              x1_ref, h2_ref, cnt_ref, e1_ref, rank_ref, e2_ref, *, alpha):
    cat = jnp.concatenate([att_ref[...], conv_ref[...]] + [sb_ref[h] for h in range(SB_HEADS)], axis=-1)
    mix = jnp.dot(cat, wo_ref[...], preferred_element_type=F32)
    x1 = _ln_noaffine(alpha * x_ref[...] + (1.0 + g1_ref[0]) * mix) * lng_ref[...] + lnb_ref[...]
    x1_ref[...] = x1
    h2 = (_ln_noaffine(x1) * (1.0 + sc2_ref[0]) + sh2_ref[0]).astype(BF16)
    h2_ref[...] = h2
    qp = jnp.dot(h2, wq_ref[...], preferred_element_type=F32).astype(BF16)
    _route(qp, k1_ref, k2_ref, cnt_ref, e1_ref, rank_ref, e2_ref)


def _route(qp, k1_ref, k2_ref, cnt_ref, e1_ref, rank_ref, e2_ref):
    for h in range(PEER_HEADS):
        o = h * 2 * PEER_HALF
        s1 = lax.dot_general(k1_ref[h], qp[:, o:o + PEER_HALF], _NT, preferred_element_type=F32)
        s2 = lax.dot_general(k2_ref[h], qp[:, o + PEER_HALF:o + 2 * PEER_HALF], _NT,
                             preferred_element_type=F32)
        split = lambda s: [s[SUBLANES * i:SUBLANES * (i + 1), :] for i in range(PEER_N_KEYS // SUBLANES)]
        v1 = _top16_sorted(split(s1))
        v2 = _top16_sorted(split(s2))
        top = _top16_pair_sums(v1, v2)
        tau = top[PEER_TOPK - 1][0:1, :]
        z = jnp.ones_like(top[0])
        for t in top[1:]:
            z = z + jnp.exp(t - top[0])
        e1_ref[h] = jnp.exp(s1 - v1[0][0:1, :]).astype(BF16)
        e2_ref[h] = (jnp.exp(s2 - v2[0][0:1, :]) * (0.5 / z[0:1, :])).astype(BF16)
        col = [v[0:1, :] for v in v2]
        cnt_ref[h] = _count_prefix(lambda vb: s1 + vb >= tau, col).astype(BF16)
        rank_ref[h] = _count_prefix(lambda vb: s2 < vb, col).astype(BF16)


def _mid(att, conv, sb, x, g1, sh2, sc2, lw, B, S, tm, alpha):
    T, D = x.shape
    nt = S // tm
    row = lambda i: (i, 0)
    per_b = lambda i: (i // nt, 0, 0)
    const2 = lambda i: (0, 0)
    const3 = lambda i: (0, 0, 0)
    route = pl.BlockSpec((PEER_HEADS, PEER_N_KEYS, tm), lambda i: (0, 0, i))
    route_shape = jax.ShapeDtypeStruct((PEER_HEADS, PEER_N_KEYS, T), BF16)
    return pl.pallas_call(
        functools.partial(_mid_body, alpha=alpha),
        grid=(T // tm,),
        in_specs=[pl.BlockSpec((tm, att.shape[1]), row), pl.BlockSpec((tm, CONV_CH), row),
                  pl.BlockSpec((SB_HEADS, tm, SB_HEAD_DIM), lambda i: (0, i, 0)),
                  pl.BlockSpec((tm, D), row),
                  pl.BlockSpec((1, 1, D), per_b), pl.BlockSpec((1, 1, D), per_b),
                  pl.BlockSpec((1, 1, D), per_b),
                  pl.BlockSpec((1, D), const2), pl.BlockSpec((1, D), const2),
                  pl.BlockSpec(lw["w_o"].shape, const2), pl.BlockSpec(lw["wq"].shape, const2),
                  pl.BlockSpec(lw["k1"].shape, const3), pl.BlockSpec(lw["k2"].shape, const3)],
        out_specs=[pl.BlockSpec((tm, D), row), pl.BlockSpec((tm, D), row), route, route, route, route],
        out_shape=[jax.ShapeDtypeStruct((T, D), F32), jax.ShapeDtypeStruct((T, D), BF16),
                   route_shape, route_shape, route_shape, route_shape],
        compiler_params=_params("arbitrary"),
        name="mid",
    )(att, conv, sb, x, g1, sh2, sc2, lw["ln1_g"], lw["ln1_b"], lw["w_o"], lw["wq"],
      lw["k1"], lw["k2"])


def _peer_body(h2_ref, u_ref, vt_ref, cnt_ref, e1_ref, rank_ref, e2_ref, x1_ref, g2_ref, lng_ref,
               lnb_ref, o_ref, a_ref, w_ref, acc_ref, *, tm, rows, alpha):
    j = pl.program_id(1)

    @pl.when(j == 0)
    def _():
        acc_ref[...] = jnp.zeros_like(acc_ref)

    for e0 in range(0, rows * PEER_N_KEYS, MXU_TILE_ROWS):
        for t0 in range(0, tm, MXU_TILE_LANES):
            a_ref[e0:e0 + MXU_TILE_ROWS, t0:t0 + MXU_TILE_LANES] = lax.dot_general(
                u_ref[e0:e0 + MXU_TILE_ROWS, :], h2_ref[t0:t0 + MXU_TILE_LANES, :], _NT,
                preferred_element_type=F32).astype(BF16)

    for rr in range(rows):
        rs = slice(rr * PEER_N_KEYS, (rr + 1) * PEER_N_KEYS)
        tw = 2 * LANES
        for ts in range(tm // tw):
            ln = slice(ts * tw, (ts + 1) * tw)
            g = None
            for h in range(PEER_HEADS):
                cnt = cnt_ref[h, rr:rr + 1, ln]
                e1 = e1_ref[h, rr:rr + 1, ln]
                e2 = e2_ref[h, :, ln]
                term = jnp.where(rank_ref[h, :, ln] < cnt, e2, jnp.zeros_like(e2)) * e1
                g = term if g is None else g + term
            a = a_ref[rs, ln]
            act = a * (1.0 + lax.erf(a * SQRT_HALF))
            w_ref[rs, ln] = act * g
    acc_ref[...] += jnp.dot(vt_ref[...], w_ref[...], preferred_element_type=F32)

    @pl.when(j == pl.num_programs(1) - 1)
    def _():
        ffn = acc_ref[...].T
        y = alpha * x1_ref[...] + (1.0 + g2_ref[0]) * ffn
        o_ref[...] = _ln_noaffine(y) * lng_ref[...] + lnb_ref[...]


def _peer(h2, cnt, e1, rank, e2, x1, g2, lw, B, S, tm, rows, alpha):
    T, D = x1.shape
    nt = S // tm
    ne = rows * PEER_N_KEYS
    n_steps = PEER_N_KEYS // rows
    row = lambda i, j: (i, 0)
    const2 = lambda i, j: (0, 0)
    return pl.pallas_call(
        functools.partial(_peer_body, tm=tm, rows=rows, alpha=alpha),
        grid=(T // tm, n_steps),
        in_specs=[pl.BlockSpec((tm, D), row),
                  pl.BlockSpec((ne, D), lambda i, j: (j, 0)),
                  pl.BlockSpec((D, ne), lambda i, j: (0, j)),
                  pl.BlockSpec((PEER_HEADS, rows, tm), lambda i, j: (0, j, i)),
                  pl.BlockSpec((PEER_HEADS, rows, tm), lambda i, j: (0, j, i)),
                  pl.BlockSpec((PEER_HEADS, PEER_N_KEYS, tm), lambda i, j: (0, 0, i)),
                  pl.BlockSpec((PEER_HEADS, PEER_N_KEYS, tm), lambda i, j: (0, 0, i)),
                  pl.BlockSpec((tm, D), row),
                  pl.BlockSpec((1, 1, D), lambda i, j: (i // nt, 0, 0)),
                  pl.BlockSpec((1, D), const2), pl.BlockSpec((1, D), const2)],
        out_specs=pl.BlockSpec((tm, D), row),
        out_shape=jax.ShapeDtypeStruct((T, D), F32),
        scratch_shapes=[pltpu.VMEM((ne, tm), BF16), pltpu.VMEM((ne, tm), BF16),
                        pltpu.VMEM((D, tm), F32)],
        compiler_params=_params("arbitrary", "arbitrary"),
        name="peer",
    )(h2, lw["u"], lw["vt"], cnt, e1, rank, e2, x1, g2, lw["ln2_g"], lw["ln2_b"])


def _layer_weights(l, w_in, q_norm_g, kv_norm_g, w_uq, w_ukv, conv_w, w_o, ln1_g, ln1_b,
                   peer_wq, peer_k1, peer_k2, peer_u, peer_v, ln2_g, ln2_b):
    D = D_MODEL
    wi = w_in[l]
    pad = jnp.zeros((D, LANES - MLA_ROPE), F32)
    kr_end = MLA_Q_RANK + MLA_KV_RANK + MLA_ROPE
    w_in_p = jnp.concatenate([wi[:, :kr_end], pad, wi[:, kr_end:]], axis=1).astype(BF16)
    qk = MLA_NOPE + MLA_ROPE
    w_uq_p = jnp.pad(w_uq[l].reshape(MLA_Q_RANK, MLA_HEADS, qk),
                     ((0, 0), (0, 0), (0, HEAD_PAD - qk))).reshape(MLA_Q_RANK, -1).astype(BF16)
    ukv = w_ukv[l].reshape(MLA_KV_RANK, MLA_HEADS, MLA_NOPE + MLA_V)
    wk = jnp.pad(ukv[:, :, :MLA_NOPE], ((0, 0), (0, 0), (0, HEAD_PAD - MLA_NOPE)))
    expand = np.zeros((LANES, MLA_HEADS, HEAD_PAD), np.float32)
    for r in range(MLA_ROPE):
        expand[r, :, MLA_NOPE + r] = 1.0
    wk2 = jnp.concatenate([wk, jnp.asarray(expand)], axis=0).reshape(MLA_KV_RANK + LANES, -1).astype(BF16)
    wv = ukv[:, :, MLA_NOPE:].reshape(MLA_KV_RANK, -1).astype(BF16)

    def rotary_partner(w):
        w3 = w.reshape(w.shape[0], MLA_HEADS, HEAD_PAD)
        half = MLA_ROPE // 2
        a, b, c = MLA_NOPE, MLA_NOPE + half, MLA_NOPE + MLA_ROPE
        out = jnp.zeros_like(w3).at[:, :, a:b].set(-w3[:, :, b:c]).at[:, :, b:c].set(w3[:, :, a:b])
        return out.reshape(w.shape)
    return dict(
        w_in=w_in_p, qg=q_norm_g[l].reshape(1, -1), kvg=kv_norm_g[l].reshape(1, -1),
        w_uq=w_uq_p, w_uq_rot=rotary_partner(w_uq_p), wk2=wk2, wk2_rot=rotary_partner(wk2), wv=wv,
        conv_w=conv_w[l],
        w_o=w_o[l].astype(BF16), ln1_g=ln1_g[l].reshape(1, D), ln1_b=ln1_b[l].reshape(1, D),
        wq=peer_wq[l].astype(BF16), k1=peer_k1[l].astype(BF16), k2=peer_k2[l].astype(BF16),
        u=peer_u[l].astype(BF16), vt=peer_v[l].T.astype(BF16),
        ln2_g=ln2_g[l].reshape(1, D), ln2_b=ln2_b[l].reshape(1, D))


def _forward(x, c, positions, ada_w, ada_b, *weights, tm_in, tq_mla, t_sb, tm_mid, tm_peer, rows):
    B, S, D = x.shape
    T = B * S
    depth = ada_w.shape[0]
    alpha = (2.0 * depth) ** 0.25
    mod = _adaln(c, ada_w, ada_b)
    half = MLA_ROPE // 2
    inv_freq = ROPE_THETA ** (-jnp.arange(half, dtype=F32) / half)
    invf = jnp.zeros((LANES,), F32).at[MLA_NOPE:MLA_NOPE + half].set(inv_freq)
    invf = invf.at[MLA_NOPE + half:MLA_NOPE + MLA_ROPE].set(inv_freq).reshape(1, LANES)
    cos, sin = _rope_tables(positions.reshape(T, 1), invf, S)
    xf = x.reshape(T, D)
    for l in range(depth):
        lw = _layer_weights(l, *weights)
        sh1, sc1, g1, sh2, sc2, g2 = [m.reshape(B, 1, D) for m in jnp.split(mod[l], 6, axis=-1)]
        q, k, v, conv, sbq, sbk, sbv = _mixer_in(xf, cos, sin, sh1, sc1, lw, B, S, tm_in)
        att = _mla(q, k, v, B, S, tq_mla)
        sbo = _stickbreak(sbq, sbk, sbv, B, S, t_sb)
        x1, h2, cnt, e1, rank, e2 = _mid(att, conv, sbo, xf, g1, sh2, sc2, lw, B, S, tm_mid, alpha)
        xf = _peer(h2, cnt, e1, rank, e2, x1, g2, lw, B, S, tm_peer, rows, alpha)
    return xf.reshape(B, S, D)


def kernel(x, c, positions, ada_w, ada_b, w_in, q_norm_g, kv_norm_g, w_uq, w_ukv, conv_w, w_o,
           ln1_g, ln1_b, peer_wq, peer_k1, peer_k2, peer_u, peer_v, ln2_g, ln2_b):
    return _forward(x, c, positions, ada_w, ada_b, w_in, q_norm_g, kv_norm_g, w_uq, w_ukv, conv_w,
                    w_o, ln1_g, ln1_b, peer_wq, peer_k1, peer_k2, peer_u, peer_v, ln2_g, ln2_b,
                    tm_in=512, tq_mla=512, t_sb=256, tm_mid=256, tm_peer=512, rows=16)
```
